```python
import math
import jax, jax.numpy as jnp
from jax import lax
import numpy as np

D_MODEL = 1024
BATCH = 8
SEQ = 2048
DEPTH = 2

N_MIXERS = 2
CHUNK = 128
N_MEM = 256
D_INNER = 2 * D_MODEL
A_GROUPS = 8
SSM_HEAD_DIM = 64
SSM_HEADS = D_INNER // SSM_HEAD_DIM
SSM_GROUPS = 4
SSM_HPG = SSM_HEADS // SSM_GROUPS
SSM_STATE = 128
CONV_K = 4
CONV_DIM = D_INNER + 2 * SSM_GROUPS * SSM_STATE
X_HEADS = 4
X_HEAD_DIM = 256
X_WIDTH = X_HEADS * X_HEAD_DIM
MIX_OUT = D_INNER + X_WIDTH
D_FF = 4 * D_MODEL
A_IN = 2 * D_INNER + X_WIDTH
B_IN = D_INNER + CONV_DIM + SSM_HEADS + X_WIDTH
EPS = 1e-6

kernel_name = "hybrid_gmlp_ssd_memxattn"


def rms_norm(x, g):
    xf = x.astype(jnp.float32)
    y = xf * lax.rsqrt(jnp.mean(xf * xf, axis=-1, keepdims=True) + EPS)
    return (y * g.astype(jnp.float32)).astype(x.dtype)


def layer_norm(x, g, b):
    xf = x.astype(jnp.float32)
    mu = jnp.mean(xf, axis=-1, keepdims=True)
    xc = xf - mu
    y = xc * lax.rsqrt(jnp.mean(xc * xc, axis=-1, keepdims=True) + EPS)
    return (y * g.astype(jnp.float32) + b.astype(jnp.float32)).astype(x.dtype)


def gmlp_spatial_gating(u, v, ln_g, ln_b, ws, bs):
    bn, s, _ = u.shape
    v = layer_norm(v, ln_g, ln_b)
    v = v.reshape(bn, s // CHUNK, CHUNK, A_GROUPS, D_INNER // A_GROUPS)
    causal = jnp.tril(jnp.ones((CHUNK, CHUNK), dtype=bool))
    w = jnp.where(causal[None], ws, jnp.zeros_like(ws))
    sv = jnp.einsum('gts,bcsgd->bctgd', w, v) + bs.T[:, :, None]
    return u * sv.reshape(bn, s, D_INNER)


def causal_dwconv(x, w, b):
    y = lax.conv_general_dilated(
        x, w[:, None, :], window_strides=(1,), padding=[(CONV_K - 1, 0)],
        dimension_numbers=('NWC', 'WIO', 'NWC'), feature_group_count=x.shape[-1])
    return y + b


def ssd_mixer(zxbcdt, conv_w, conv_b, dt_bias, a_log, d_skip, gnorm):
    bn, s, _ = zxbcdt.shape
    nc = s // CHUNK
    z = zxbcdt[..., :D_INNER]
    xbc = zxbcdt[..., D_INNER:D_INNER + CONV_DIM]
    dt = zxbcdt[..., D_INNER + CONV_DIM:]
    xbc = jax.nn.silu(causal_dwconv(xbc, conv_w, conv_b))
    gn = SSM_GROUPS * SSM_STATE
    xs = xbc[..., :D_INNER].astype(jnp.float32)
    bm = xbc[..., D_INNER:D_INNER + gn].astype(jnp.float32)
    cm = xbc[..., D_INNER + gn:].astype(jnp.float32)

    x = xs.reshape(bn, nc, CHUNK, SSM_GROUPS, SSM_HPG, SSM_HEAD_DIM)
    bm = bm.reshape(bn, nc, CHUNK, SSM_GROUPS, SSM_STATE)
    cm = cm.reshape(bn, nc, CHUNK, SSM_GROUPS, SSM_STATE)
    dt = jax.nn.softplus(dt.astype(jnp.float32) + dt_bias.astype(jnp.float32))
    dt = dt.reshape(bn, nc, CHUNK, SSM_GROUPS, SSM_HPG)
    a = -jnp.exp(a_log.astype(jnp.float32)).reshape(SSM_GROUPS, SSM_HPG)
    da = jnp.transpose(dt * a, (0, 3, 4, 1, 2))
    xdt = x * dt[..., None]

    cs = jnp.cumsum(da, axis=-1)
    causal = jnp.tril(jnp.ones((CHUNK, CHUNK), dtype=bool))
    seg = cs[..., :, None] - cs[..., None, :]
    lmat = jnp.exp(jnp.where(causal, seg, -jnp.inf))

    cb = jnp.einsum('bclgn,bcsgn->bcgls', cm, bm)
    y_diag = jnp.einsum('bcgls,bgrcls,bcsgrp->bclgrp', cb, lmat, xdt)

    decay_states = jnp.exp(cs[..., -1:] - cs)
    states = jnp.einsum('bclgn,bgrcl,bclgrp->bcgrpn', bm, decay_states, xdt)
    chunk_decay = jnp.exp(cs[..., -1])

    def step(h, inp):
        st, dec = inp
        return h * dec[..., None, None] + st, h

    h0 = jnp.zeros((bn, SSM_GROUPS, SSM_HPG, SSM_HEAD_DIM, SSM_STATE), jnp.float32)
    _, prev = lax.scan(step, h0, (jnp.moveaxis(states, 1, 0), jnp.moveaxis(chunk_decay, 3, 0)))
    prev = jnp.moveaxis(prev, 0, 1)

    y_off = jnp.einsum('bclgn,bcgrpn,bgrcl->bclgrp', cm, prev, jnp.exp(cs))
    y = y_diag + y_off + x * d_skip.astype(jnp.float32).reshape(SSM_GROUPS, SSM_HPG)[..., None]
    y = y.reshape(bn, s, D_INNER)

    yg = (y * jax.nn.silu(z.astype(jnp.float32))).reshape(bn, s, SSM_GROUPS, D_INNER // SSM_GROUPS)
    yg = yg * lax.rsqrt(jnp.mean(yg * yg, axis=-1, keepdims=True) + EPS)
    y = yg.reshape(bn, s, D_INNER) * gnorm.astype(jnp.float32)
    return y.astype(zxbcdt.dtype)


def memory_attention(q, mem, mem_g, w_kv):
    bn, s, _ = q.shape
    m = rms_norm(mem, mem_g)
    kv = m @ w_kv
    k = kv[..., :X_WIDTH].reshape(bn, N_MEM, X_HEADS, X_HEAD_DIM)
    v = kv[..., X_WIDTH:].reshape(bn, N_MEM, X_HEADS, X_HEAD_DIM)
    qh = q.reshape(bn, s, X_HEADS, X_HEAD_DIM)
    sc = jnp.einsum('bshd,bmhd->bhsm', qh, k).astype(jnp.float32) * (1.0 / math.sqrt(X_HEAD_DIM))
    p = jax.nn.softmax(sc, axis=-1).astype(v.dtype)
    o = jnp.einsum('bhsm,bmhd->bshd', p, v)
    return o.reshape(bn, s, X_WIDTH)


def setup_inputs(seed: int = 0) -> dict:
    key = jax.random.key(seed)
    ks = jax.random.split(key, 32)
    na = (DEPTH + 1) // 2
    nb = DEPTH // 2
    f32 = jnp.float32

    def nrm(k, shape, scale):
        return jax.random.normal(k, shape, f32) * scale

    def gain(k, shape):
        return 1.0 + 0.02 * jax.random.normal(k, shape, f32)

    dt0 = jnp.exp(jax.random.uniform(ks[20], (nb, SSM_HEADS), f32, math.log(1e-3), math.log(1e-1)))
    return {
        "x": jax.random.normal(ks[0], (BATCH, SEQ, D_MODEL), f32),
        "mem": jax.random.normal(ks[1], (BATCH, N_MEM, D_MODEL), f32),
        "norm_mix": gain(ks[2], (DEPTH, D_MODEL)),
        "norm_ffn": gain(ks[3], (DEPTH, D_MODEL)),
        "mem_norm": gain(ks[4], (DEPTH, D_MODEL)),
        "w_kv": nrm(ks[5], (DEPTH, D_MODEL, 2 * X_WIDTH), D_MODEL ** -0.5),
        "w_out": nrm(ks[6], (DEPTH, MIX_OUT, D_MODEL), MIX_OUT ** -0.5),
        "w_ffn1": nrm(ks[7], (DEPTH, D_MODEL, D_FF), D_MODEL ** -0.5),
        "w_ffn2": nrm(ks[8], (DEPTH, D_FF, D_MODEL), D_FF ** -0.5),
        "a_in": nrm(ks[9], (na, D_MODEL, A_IN), D_MODEL ** -0.5),
        "a_ln_g": gain(ks[10], (na, D_INNER)),
        "a_ln_b": nrm(ks[11], (na, D_INNER), 0.02),
        "a_ws": nrm(ks[12], (na, A_GROUPS, CHUNK, CHUNK), 0.5 * CHUNK ** -0.5),
        "a_bs": gain(ks[13], (na, A_GROUPS, CHUNK)),
        "b_in": nrm(ks[14], (nb, D_MODEL, B_IN), D_MODEL ** -0.5),
        "b_conv_w": nrm(ks[15], (nb, CONV_K, CONV_DIM), CONV_K ** -0.5),
        "b_conv_b": nrm(ks[16], (nb, CONV_DIM), 0.02),
        "b_dt_bias": dt0 + jnp.log(-jnp.expm1(-dt0)),
        "b_a_log": jnp.log(jax.random.uniform(ks[17], (nb, SSM_HEADS), f32, 1.0, 16.0)),
        "b_d": gain(ks[18], (nb, SSM_HEADS)),
        "b_gnorm": gain(ks[19], (nb, D_INNER)),
        "final_norm": gain(ks[21], (D_MODEL,)),
    }


def reference(x, mem, norm_mix, norm_ffn, mem_norm, w_kv, w_out, w_ffn1, w_ffn2,
              a_in, a_ln_g, a_ln_b, a_ws, a_bs,
              b_in, b_conv_w, b_conv_b, b_dt_bias, b_a_log, b_d, b_gnorm,
              final_norm):
    h = x
    for i in range(DEPTH):
        j = i // N_MIXERS
        a = rms_norm(h, norm_mix[i])
        if i % N_MIXERS == 0:
            proj = a @ a_in[j]
            u = jax.nn.gelu(proj[..., :D_INNER], approximate=False)
            v = jax.nn.gelu(proj[..., D_INNER:2 * D_INNER], approximate=False)
            q = proj[..., 2 * D_INNER:]
            mix = gmlp_spatial_gating(u, v, a_ln_g[j], a_ln_b[j], a_ws[j], a_bs[j])
        else:
            proj = a @ b_in[j]
            ssm_cols = D_INNER + CONV_DIM + SSM_HEADS
            mix = ssd_mixer(proj[..., :ssm_cols], b_conv_w[j], b_conv_b[j], b_dt_bias[j],
                            b_a_log[j], b_d[j], b_gnorm[j])
            q = proj[..., ssm_cols:]
        mo = memory_attention(q, mem, mem_norm[i], w_kv[i])
        h = h + jnp.concatenate([mix, mo], axis=-1) @ w_out[i]
        f = rms_norm(h, norm_ffn[i])
        h = h + jnp.square(jax.nn.relu(f @ w_ffn1[i])) @ w_ffn2[i]
    return rms_norm(h, final_norm)
```

```python
import functools
import math

import jax
import jax.numpy as jnp
from jax import lax
from jax.experimental import pallas as pl
from jax.experimental.pallas import tpu as pltpu

F32 = jnp.float32
BF16 = jnp.bfloat16

D_MODEL = 1024
N_MIXERS = 2
CHUNK = 128
N_MEM = 256
D_INNER = 2 * D_MODEL
A_GROUPS = 8
A_GROUP_W = D_INNER // A_GROUPS
SSM_HEAD_DIM = 64
SSM_HEADS = D_INNER // SSM_HEAD_DIM
SSM_GROUPS = 4
SSM_HPG = SSM_HEADS // SSM_GROUPS
SSM_STATE = 128
SSM_GROUP_W = SSM_HPG * SSM_HEAD_DIM
CONV_K = 4
CONV_DIM = D_INNER + 2 * SSM_GROUPS * SSM_STATE
X_HEADS = 4
X_HEAD_DIM = 256
X_WIDTH = X_HEADS * X_HEAD_DIM
MIX_OUT = D_INNER + X_WIDTH
D_FF = 4 * D_MODEL
EPS = 1e-6

LANES = 128
SUBLANES = 8
PAIR_W = 2 * SSM_HEAD_DIM
assert PAIR_W == LANES and SSM_STATE == LANES and CHUNK == LANES

TOKEN_TILE = 256
FF_COLS = 1024
VMEM_LIMIT = 56 * 1024 * 1024


def _mm(a, b):
    return jnp.dot(a, b, preferred_element_type=F32)


def _mm_nt(a, b):
    return lax.dot_general(a, b, (((1,), (1,)), ((), ())), preferred_element_type=F32)


def _mm_f32(a, b):
    return jnp.dot(a, b, preferred_element_type=F32, precision=lax.Precision.HIGHEST)


def _rms(x, g):
    return x * lax.rsqrt(jnp.mean(x * x, axis=-1, keepdims=True) + EPS) * g


def _gelu(x):
    return 0.5 * x * (1.0 + lax.erf(x * math.sqrt(0.5)))


def _softplus(x):
    return jnp.maximum(x, 0.0) + jnp.log1p(jnp.exp(-jnp.abs(x)))


def _resident(shape):
    zeros = (0,) * len(shape)
    return pl.BlockSpec(shape, lambda *_: zeros, pipeline_mode=pl.Buffered(1))


def _mem_attention(q, kt_ref, v_ref, o_ref, col0):
    qb = q.astype(BF16)
    for hh in range(X_HEADS):
        sl = slice(hh * X_HEAD_DIM, (hh + 1) * X_HEAD_DIM)
        sc = _mm(qb[:, sl], kt_ref[sl, :])
        e = jnp.exp(sc - jnp.max(sc, axis=-1, keepdims=True))
        o = _mm(e.astype(BF16), v_ref[:, sl]) / jnp.sum(e, axis=-1, keepdims=True)
        o_ref[:, col0 + hh * X_HEAD_DIM:col0 + (hh + 1) * X_HEAD_DIM] = o.astype(o_ref.dtype)


def _kv_kernel(mem_ref, g_ref, w_ref, kt_ref, v_ref):
    m = _rms(mem_ref[...], g_ref[...]).astype(BF16)
    kv = _mm(m, w_ref[...])
    scale = 1.0 / math.sqrt(X_HEAD_DIM)
    kt_ref[...] = (kv[:, :X_WIDTH] * scale).T.astype(BF16)
    v_ref[...] = kv[:, X_WIDTH:].astype(BF16)


def _memory_kv(mem, mem_norm, w_kv):
    depth = w_kv.shape[0]
    bn = mem.shape[0]
    return pl.pallas_call(
        _kv_kernel,
        grid=(depth, bn),
        in_specs=[
            pl.BlockSpec((None, N_MEM, D_MODEL), lambda l, b: (b, 0, 0)),
            pl.BlockSpec((None, 1, D_MODEL), lambda l, b: (l, 0, 0)),
            pl.BlockSpec((None, D_MODEL, 2 * X_WIDTH), lambda l, b: (l, 0, 0)),
        ],
        out_specs=[
            pl.BlockSpec((None, None, X_WIDTH, N_MEM), lambda l, b: (l, b, 0, 0)),
            pl.BlockSpec((None, None, N_MEM, X_WIDTH), lambda l, b: (l, b, 0, 0)),
        ],
        out_shape=[
            jax.ShapeDtypeStruct((depth, bn, X_WIDTH, N_MEM), BF16),
            jax.ShapeDtypeStruct((depth, bn, N_MEM, X_WIDTH), BF16),
        ],
        compiler_params=pltpu.CompilerParams(
            dimension_semantics=("parallel", "parallel"), vmem_limit_bytes=VMEM_LIMIT),
        name="kv",
    )(mem, mem_norm.reshape(depth, 1, D_MODEL), w_kv.astype(BF16))


def _mix_a_kernel(h_ref, g_ref, w_ref, lng_ref, lnb_ref, ws_ref, bst_ref, kt_ref, v_ref, o_ref):
    rows = h_ref.shape[0]
    a = _rms(h_ref[...], g_ref[...]).astype(BF16)
    u = _gelu(_mm(a, w_ref[:, 0:D_INNER]))
    v = _gelu(_mm(a, w_ref[:, D_INNER:2 * D_INNER]))
    vc = v - jnp.mean(v, axis=-1, keepdims=True)
    vn = vc * lax.rsqrt(jnp.mean(vc * vc, axis=-1, keepdims=True) + EPS)
    vn = (vn * lng_ref[...] + lnb_ref[...]).astype(BF16)

    t_idx = lax.broadcasted_iota(jnp.int32, (CHUNK, CHUNK), 0)
    s_idx = lax.broadcasted_iota(jnp.int32, (CHUNK, CHUNK), 1)
    causal = t_idx >= s_idx
    for g in range(A_GROUPS):
        cols = slice(g * A_GROUP_W, (g + 1) * A_GROUP_W)
        wg = jnp.where(causal, ws_ref[g], 0.0).astype(BF16)
        bias = bst_ref[:, g:g + 1]
        for c in range(rows // CHUNK):
            rsl = slice(c * CHUNK, (c + 1) * CHUNK)
            sv = _mm(wg, vn[rsl, cols]) + bias
            o_ref[rsl, cols] = (u[rsl, cols] * sv).astype(o_ref.dtype)

    q = _mm(a, w_ref[:, 2 * D_INNER:2 * D_INNER + X_WIDTH])
    _mem_attention(q, kt_ref, v_ref, o_ref, D_INNER)


def _mix_a(h, bn, g, w_in, ln_g, ln_b, ws, bs, kt, vv):
    t = h.shape[0]
    tm = TOKEN_TILE
    tiles_per_batch = (t // bn) // tm
    return pl.pallas_call(
        _mix_a_kernel,
        grid=(t // tm,),
        in_specs=[
            pl.BlockSpec((tm, D_MODEL), lambda i: (i, 0)),
            _resident((1, D_MODEL)),
            _resident((D_MODEL, 2 * D_INNER + X_WIDTH)),
            _resident((1, D_INNER)),
            _resident((1, D_INNER)),
            _resident((A_GROUPS, CHUNK, CHUNK)),
            _resident((CHUNK, A_GROUPS)),
            pl.BlockSpec((None, X_WIDTH, N_MEM), lambda i: (i // tiles_per_batch, 0, 0)),
            pl.BlockSpec((None, N_MEM, X_WIDTH), lambda i: (i // tiles_per_batch, 0, 0)),
        ],
        out_specs=pl.BlockSpec((tm, MIX_OUT), lambda i: (i, 0)),
        out_shape=jax.ShapeDtypeStruct((t, MIX_OUT), BF16),
        compiler_params=pltpu.CompilerParams(
            dimension_semantics=("parallel",), vmem_limit_bytes=VMEM_LIMIT),
        name="mix_a",
    )(h, g.reshape(1, D_MODEL), w_in.astype(BF16), ln_g.reshape(1, D_INNER), ln_b.reshape(1, D_INNER),
      ws, bs.T, kt, vv)


def _mix_b_kernel(h_ref, g_ref, wz_ref, wx_ref, wdt_ref, wdtt_ref, wq_ref, cw_ref, cb_ref,
                  dtb_ref, dtbt_ref, alog_ref, alogt_ref, dexp_ref, gn_ref, kt_ref, v_ref,
                  o_ref, state_ref, ext_ref, y_ref):
    rows = h_ref.shape[0]

    @pl.when(pl.program_id(1) == 0)
    def _():
        state_ref[...] = jnp.zeros_like(state_ref)
        ext_ref[0:SUBLANES, :] = jnp.zeros((SUBLANES, CONV_DIM), F32)

    a = _rms(h_ref[...], g_ref[...]).astype(BF16)
    z = _mm(a, wz_ref[...])

    ext_ref[SUBLANES:SUBLANES + rows, :] = _mm(a, wx_ref[...])
    acc = ext_ref[pl.ds(SUBLANES - CONV_K + 1, rows), :] * cw_ref[0:1, :]
    for k in range(1, CONV_K):
        acc = acc + ext_ref[pl.ds(SUBLANES - CONV_K + 1 + k, rows), :] * cw_ref[k:k + 1, :]
    ext_ref[0:SUBLANES, :] = ext_ref[rows:rows + SUBLANES, :]
    xbc = jax.nn.silu(acc + cb_ref[...])

    dt = _softplus(_mm(a, wdt_ref[...]) + dtb_ref[...])
    dtt = _softplus(_mm_nt(wdtt_ref[...], a) + dtbt_ref[...])
    da = dt * -jnp.exp(alog_ref[...])
    dat = dtt * -jnp.exp(alogt_ref[...])

    l_idx = lax.broadcasted_iota(jnp.int32, (CHUNK, CHUNK), 0)
    s_idx = lax.broadcasted_iota(jnp.int32, (CHUNK, CHUNK), 1)
    causal = l_idx >= s_idx
    lower_ones = jnp.where(causal, 1.0, 0.0).astype(F32)
    upper_ones = jnp.where(l_idx <= s_idx, 1.0, 0.0).astype(F32)
    first_head = s_idx < SSM_HEAD_DIM
    first_head_row = first_head[0:1, :]

    for c in range(rows // CHUNK):
        rsl = slice(c * CHUNK, (c + 1) * CHUNK)
        cs = _mm_f32(lower_ones, da[rsl, :])
        cst = _mm_f32(dat[:, rsl], upper_ones)
        cs_last = cs[CHUNK - 1:CHUNK, :]
        dstate = jnp.exp(cs_last - cs)
        ecs = jnp.exp(cs)
        cdec = jnp.exp(cs_last)
        dt_c = dt[rsl, :]

        for g in range(SSM_GROUPS):
            b0 = D_INNER + g * SSM_STATE
            c0 = D_INNER + SSM_GROUPS * SSM_STATE + g * SSM_STATE
            bm = xbc[rsl, b0:b0 + SSM_STATE]
            cm = xbc[rsl, c0:c0 + SSM_STATE].astype(BF16)
            cb = _mm_nt(cm, bm.astype(BF16))
            bmt = bm.T.astype(BF16)

            for r in range(SSM_HPG // 2):
                p = g * (SSM_HPG // 2) + r
                h0, h1 = 2 * p, 2 * p + 1
                cols = slice(p * PAIR_W, (p + 1) * PAIR_W)
                pcols = slice(r * PAIR_W, (r + 1) * PAIR_W)

                def expand(v):
                    return jnp.where(first_head, v[:, h0:h0 + 1], v[:, h1:h1 + 1])

                xs = xbc[rsl, cols]
                xdt = xs * expand(dt_c)
                xdt_b = xdt.astype(BF16)

                seg0 = cs[:, h0:h0 + 1] - cst[h0:h0 + 1, :]
                seg1 = cs[:, h1:h1 + 1] - cst[h1:h1 + 1, :]
                m0 = (cb * jnp.exp(jnp.where(causal, seg0, -jnp.inf))).astype(BF16)
                m1 = (cb * jnp.exp(jnp.where(causal, seg1, -jnp.inf))).astype(BF16)
                lhs = jnp.concatenate([m0, m1], axis=1)
                zero = jnp.zeros_like(xdt_b)
                rhs = jnp.concatenate([jnp.where(first_head, xdt_b, zero),
                                       jnp.where(first_head, zero, xdt_b)], axis=0)
                y_diag = _mm(lhs, rhs)

                st = state_ref[g, :, pcols]
                y_off = _mm(cm, st.astype(BF16)) * expand(ecs)
                y_ref[rsl, cols] = y_diag + y_off + xs * dexp_ref[:, cols]

                dec = jnp.where(first_head_row, cdec[:, h0:h0 + 1], cdec[:, h1:h1 + 1])
                xds = (xdt * expand(dstate)).astype(BF16)
                state_ref[g, :, pcols] = st * dec + _mm(bmt, xds)

    gw = D_INNER // SSM_GROUPS
    for g in range(SSM_GROUPS):
        cols = slice(g * gw, (g + 1) * gw)
        yg = y_ref[:, cols] * jax.nn.silu(z[:, cols])
        yg = yg * lax.rsqrt(jnp.mean(yg * yg, axis=-1, keepdims=True) + EPS)
        o_ref[:, cols] = (yg * gn_ref[:, cols]).astype(o_ref.dtype)

    q = _mm(a, wq_ref[...])
    _mem_attention(q, kt_ref, v_ref, o_ref, D_INNER)


def _mix_b(h, g, w_in, conv_w, conv_b, dt_bias, a_log, d_skip, gnorm, kt, vv):
    bn, s, _ = h.shape
    tm = TOKEN_TILE
    w = w_in.astype(BF16)
    z1 = D_INNER
    x1 = z1 + CONV_DIM
    d1 = x1 + SSM_HEADS
    wz, wx, wdt, wq = w[:, :z1], w[:, z1:x1], w[:, x1:d1], w[:, d1:]
    return pl.pallas_call(
        _mix_b_kernel,
        grid=(bn, s // tm),
        in_specs=[
            pl.BlockSpec((None, tm, D_MODEL), lambda b, i: (b, i, 0)),
            _resident((1, D_MODEL)),
            _resident((D_MODEL, D_INNER)),
            _resident((D_MODEL, CONV_DIM)),
            _resident((D_MODEL, SSM_HEADS)),
            _resident((SSM_HEADS, D_MODEL)),
            _resident((D_MODEL, X_WIDTH)),
            _resident((CONV_K, CONV_DIM)),
            _resident((1, CONV_DIM)),
            _resident((1, SSM_HEADS)),
            _resident((SSM_HEADS, 1)),
            _resident((1, SSM_HEADS)),
            _resident((SSM_HEADS, 1)),
            _resident((1, D_INNER)),
            _resident((1, D_INNER)),
            pl.BlockSpec((None, X_WIDTH, N_MEM), lambda b, i: (b, 0, 0)),
            pl.BlockSpec((None, N_MEM, X_WIDTH), lambda b, i: (b, 0, 0)),
        ],
        out_specs=pl.BlockSpec((None, tm, MIX_OUT), lambda b, i: (b, i, 0)),
        out_shape=jax.ShapeDtypeStruct((bn, s, MIX_OUT), BF16),
        scratch_shapes=[
            pltpu.VMEM((SSM_GROUPS, SSM_STATE, SSM_GROUP_W), F32),
            pltpu.VMEM((tm + 2 * SUBLANES, CONV_DIM), F32),
            pltpu.VMEM((tm, D_INNER), F32),
        ],
        compiler_params=pltpu.CompilerParams(
            dimension_semantics=("arbitrary", "arbitrary"), vmem_limit_bytes=VMEM_LIMIT),
        name="mix_b",
    )(h, g.reshape(1, D_MODEL), wz, wx, wdt, wdt.T, wq, conv_w, conv_b.reshape(1, CONV_DIM),
      dt_bias.reshape(1, SSM_HEADS), dt_bias.reshape(SSM_HEADS, 1),
      a_log.reshape(1, SSM_HEADS), a_log.reshape(SSM_HEADS, 1),
      jnp.repeat(d_skip, SSM_HEAD_DIM).reshape(1, D_INNER), gnorm.reshape(1, D_INNER), kt, vv)


def _post_kernel(h_ref, mc_ref, wo_ref, g_ref, w1_ref, w2_ref, fg_ref, o_ref, *, final):
    h = h_ref[...] + _mm(mc_ref[...], wo_ref[...])
    f = _rms(h, g_ref[...]).astype(BF16)
    for j in range(D_FF // FF_COLS):
        cols = slice(j * FF_COLS, (j + 1) * FF_COLS)
        t = jnp.square(jnp.maximum(_mm(f, w1_ref[:, cols]), 0.0)).astype(BF16)
        h = h + _mm(t, w2_ref[cols, :])
    if final:
        h = _rms(h, fg_ref[...])
    o_ref[...] = h


def _post(h, mixcat, w_out, g, w1, w2, final_g, final):
    t = h.shape[0]
    tm = TOKEN_TILE
    return pl.pallas_call(
        functools.partial(_post_kernel, final=final),
        grid=(t // tm,),
        in_specs=[
            pl.BlockSpec((tm, D_MODEL), lambda i: (i, 0)),
            pl.BlockSpec((tm, MIX_OUT), lambda i: (i, 0)),
            _resident((MIX_OUT, D_MODEL)),
            _resident((1, D_MODEL)),
            _resident((D_MODEL, D_FF)),
            _resident((D_FF, D_MODEL)),
            _resident((1, D_MODEL)),
        ],
        out_specs=pl.BlockSpec((tm, D_MODEL), lambda i: (i, 0)),
        out_shape=jax.ShapeDtypeStruct((t, D_MODEL), F32),
        compiler_params=pltpu.CompilerParams(
            dimension_semantics=("parallel",), vmem_limit_bytes=VMEM_LIMIT),
        name="post_final" if final else "post",
    )(h, mixcat, w_out.astype(BF16), g.reshape(1, D_MODEL), w1.astype(BF16), w2.astype(BF16),
      final_g.reshape(1, D_MODEL))


def kernel(x, mem, norm_mix, norm_ffn, mem_norm, w_kv, w_out, w_ffn1, w_ffn2,
           a_in, a_ln_g, a_ln_b, a_ws, a_bs,
           b_in, b_conv_w, b_conv_b, b_dt_bias, b_a_log, b_d, b_gnorm,
           final_norm):
    bn, s, d = x.shape
    depth = w_out.shape[0]
    assert d == D_MODEL and s % TOKEN_TILE == 0 and TOKEN_TILE % CHUNK == 0
    kt, vv = _memory_kv(mem, mem_norm, w_kv)
    h = x.reshape(bn * s, d)
    for i in range(depth):
        j = i // N_MIXERS
        if i % N_MIXERS == 0:
            mixcat = _mix_a(h, bn, norm_mix[i], a_in[j], a_ln_g[j], a_ln_b[j], a_ws[j], a_bs[j],
                            kt[i], vv[i])
        else:
            mixcat = _mix_b(h.reshape(bn, s, d), norm_mix[i], b_in[j], b_conv_w[j], b_conv_b[j],
                            b_dt_bias[j], b_a_log[j], b_d[j], b_gnorm[j], kt[i], vv[i])
            mixcat = mixcat.reshape(bn * s, MIX_OUT)
        h = _post(h, mixcat, w_out[i], norm_ffn[i], w_ffn1[i], w_ffn2[i], final_norm,
                  final=(i == depth - 1))
    return h.reshape(bn, s, d)
```

```python
import functools
import math

import jax
import jax.numpy as jnp
from jax import lax
from jax.experimental import pallas as pl
from jax.experimental.pallas import tpu as pltpu

F32 = jnp.float32
BF16 = jnp.bfloat16

D_MODEL = 1024
N_MIXERS = 2
CHUNK = 128
N_MEM = 256
D_INNER = 2 * D_MODEL
A_GROUPS = 8
A_GROUP_W = D_INNER // A_GROUPS
SSM_HEAD_DIM = 64
SSM_HEADS = D_INNER // SSM_HEAD_DIM
SSM_GROUPS = 4
SSM_HPG = SSM_HEADS // SSM_GROUPS
SSM_STATE = 128
SSM_GROUP_W = SSM_HPG * SSM_HEAD_DIM
CONV_K = 4
CONV_DIM = D_INNER + 2 * SSM_GROUPS * SSM_STATE
X_HEADS = 4
X_HEAD_DIM = 256
X_WIDTH = X_HEADS * X_HEAD_DIM
MIX_OUT = D_INNER + X_WIDTH
D_FF = 4 * D_MODEL
EPS = 1e-6

LANES = 128
SUBLANES = 8
PAIR_W = 2 * SSM_HEAD_DIM
assert PAIR_W == LANES and SSM_STATE == LANES and CHUNK == LANES

TOKEN_TILE = 512
SCAN_TILE = 256
FF_COLS = 1024
CONV_COLS = 512
VMEM_LIMIT = 56 * 1024 * 1024


def _mm(a, b):
    return jnp.dot(a, b, preferred_element_type=F32)


def _mm_nt(a, b):
    return lax.dot_general(a, b, (((1,), (1,)), ((), ())), preferred_element_type=F32)


def _mm_f32(a, b):
    return jnp.dot(a, b, preferred_element_type=F32, precision=lax.Precision.HIGHEST)


def _rms(x, g):
    return x * lax.rsqrt(jnp.mean(x * x, axis=-1, keepdims=True) + EPS) * g


def _gelu(x):
    return 0.5 * x * (1.0 + lax.erf(x * math.sqrt(0.5)))


def _softplus(x):
    return jnp.maximum(x, 0.0) + jnp.log1p(jnp.exp(-jnp.abs(x)))


def _resident(shape):
    zeros = (0,) * len(shape)
    return pl.BlockSpec(shape, lambda *_: zeros, pipeline_mode=pl.Buffered(1))


def _mem_attention(q, kt_ref, v_ref, o_ref, col0):
    qb = q.astype(BF16)
    for hh in range(X_HEADS):
        sl = slice(hh * X_HEAD_DIM, (hh + 1) * X_HEAD_DIM)
        sc = _mm(qb[:, sl], kt_ref[sl, :])
        e = jnp.exp(sc - jnp.max(sc, axis=-1, keepdims=True))
        o = _mm(e.astype(BF16), v_ref[:, sl]) / jnp.sum(e, axis=-1, keepdims=True)
        o_ref[:, col0 + hh * X_HEAD_DIM:col0 + (hh + 1) * X_HEAD_DIM] = o.astype(o_ref.dtype)


def _kv_kernel(mem_ref, g_ref, w_ref, kt_ref, v_ref):
    m = _rms(mem_ref[...], g_ref[...]).astype(BF16)
    kv = _mm(m, w_ref[...])
    scale = 1.0 / math.sqrt(X_HEAD_DIM)
    kt_ref[...] = (kv[:, :X_WIDTH] * scale).T.astype(BF16)
    v_ref[...] = kv[:, X_WIDTH:].astype(BF16)


def _memory_kv(mem, mem_norm, w_kv):
    depth = w_kv.shape[0]
    bn = mem.shape[0]
    return pl.pallas_call(
        _kv_kernel,
        grid=(depth, bn),
        in_specs=[
            pl.BlockSpec((None, N_MEM, D_MODEL), lambda l, b: (b, 0, 0)),
            pl.BlockSpec((None, 1, D_MODEL), lambda l, b: (l, 0, 0)),
            pl.BlockSpec((None, D_MODEL, 2 * X_WIDTH), lambda l, b: (l, 0, 0)),
        ],
        out_specs=[
            pl.BlockSpec((None, None, X_WIDTH, N_MEM), lambda l, b: (l, b, 0, 0)),
            pl.BlockSpec((None, None, N_MEM, X_WIDTH), lambda l, b: (l, b, 0, 0)),
        ],
        out_shape=[
            jax.ShapeDtypeStruct((depth, bn, X_WIDTH, N_MEM), BF16),
            jax.ShapeDtypeStruct((depth, bn, N_MEM, X_WIDTH), BF16),
        ],
        compiler_params=pltpu.CompilerParams(
            dimension_semantics=("parallel", "parallel"), vmem_limit_bytes=VMEM_LIMIT),
        name="kv",
    )(mem, mem_norm.reshape(depth, 1, D_MODEL), w_kv.astype(BF16))


def _mix_a_kernel(h_ref, g_ref, w_ref, lng_ref, lnb_ref, ws_ref, bst_ref, kt_ref, v_ref, o_ref):
    rows = h_ref.shape[0]
    a = _rms(h_ref[...], g_ref[...]).astype(BF16)
    u = _gelu(_mm(a, w_ref[:, 0:D_INNER]))
    v = _gelu(_mm(a, w_ref[:, D_INNER:2 * D_INNER]))
    vc = v - jnp.mean(v, axis=-1, keepdims=True)
    vn = vc * lax.rsqrt(jnp.mean(vc * vc, axis=-1, keepdims=True) + EPS)
    vn = (vn * lng_ref[...] + lnb_ref[...]).astype(BF16)

    t_idx = lax.broadcasted_iota(jnp.int32, (CHUNK, CHUNK), 0)
    s_idx = lax.broadcasted_iota(jnp.int32, (CHUNK, CHUNK), 1)
    causal = t_idx >= s_idx
    for g in range(A_GROUPS):
        cols = slice(g * A_GROUP_W, (g + 1) * A_GROUP_W)
        wg = jnp.where(causal, ws_ref[g], 0.0).astype(BF16)
        bias = bst_ref[:, g:g + 1]
        for c in range(rows // CHUNK):
            rsl = slice(c * CHUNK, (c + 1) * CHUNK)
            sv = _mm(wg, vn[rsl, cols]) + bias
            o_ref[rsl, cols] = (u[rsl, cols] * sv).astype(o_ref.dtype)

    q = _mm(a, w_ref[:, 2 * D_INNER:2 * D_INNER + X_WIDTH])
    _mem_attention(q, kt_ref, v_ref, o_ref, D_INNER)


def _mix_a(h, bn, g, w_in, ln_g, ln_b, ws, bs, kt, vv):
    t = h.shape[0]
    tm = TOKEN_TILE
    tiles_per_batch = (t // bn) // tm
    return pl.pallas_call(
        _mix_a_kernel,
        grid=(t // tm,),
        in_specs=[
            pl.BlockSpec((tm, D_MODEL), lambda i: (i, 0)),
            _resident((1, D_MODEL)),
            _resident((D_MODEL, 2 * D_INNER + X_WIDTH)),
            _resident((1, D_INNER)),
            _resident((1, D_INNER)),
            _resident((A_GROUPS, CHUNK, CHUNK)),
            _resident((CHUNK, A_GROUPS)),
            pl.BlockSpec((None, X_WIDTH, N_MEM), lambda i: (i // tiles_per_batch, 0, 0)),
            pl.BlockSpec((None, N_MEM, X_WIDTH), lambda i: (i // tiles_per_batch, 0, 0)),
        ],
        out_specs=pl.BlockSpec((tm, MIX_OUT), lambda i: (i, 0)),
        out_shape=jax.ShapeDtypeStruct((t, MIX_OUT), BF16),
        compiler_params=pltpu.CompilerParams(
            dimension_semantics=("parallel",), vmem_limit_bytes=VMEM_LIMIT),
        name="mix_a",
    )(h, g.reshape(1, D_MODEL), w_in.astype(BF16), ln_g.reshape(1, D_INNER), ln_b.reshape(1, D_INNER),
      ws, bs.T, kt, vv)


def _mix_b_kernel(h_ref, g_ref, wz_ref, wx_ref, wdt_ref, wdtt_ref, wq_ref, cw_ref, cb_ref,
                  dtb_ref, dtbt_ref, alog_ref, alogt_ref, dexp_ref, gn_ref, kt_ref, v_ref,
                  o_ref, state_ref, ext_ref, xbc_ref, y_ref):
    rows = h_ref.shape[0]

    @pl.when(pl.program_id(1) == 0)
    def _():
        state_ref[...] = jnp.zeros_like(state_ref)
        ext_ref[0:SUBLANES, :] = jnp.zeros((SUBLANES, CONV_DIM), F32)

    a = _rms(h_ref[...], g_ref[...]).astype(BF16)

    n_blocks = CONV_DIM // CONV_COLS
    x_blocks = D_INNER // CONV_COLS
    for j in list(range(x_blocks, n_blocks)) + list(range(x_blocks)):
        cols = slice(j * CONV_COLS, (j + 1) * CONV_COLS)
        ext_ref[SUBLANES:SUBLANES + rows, cols] = _mm(a, wx_ref[:, cols])
        x = ext_ref[:, cols]
        acc = x[SUBLANES:] * cw_ref[CONV_K - 1:CONV_K, cols]
        for s in range(1, CONV_K):
            acc = acc + pltpu.roll(x, s, 0)[SUBLANES:] * cw_ref[CONV_K - 1 - s:CONV_K - s, cols]
        xbc_ref[:, cols] = jax.nn.silu(acc + cb_ref[:, cols])
        ext_ref[0:SUBLANES, cols] = x[rows:]

    dt = _softplus(_mm(a, wdt_ref[...]) + dtb_ref[...])
    dtt = _softplus(_mm_nt(wdtt_ref[...], a) + dtbt_ref[...])
    da = dt * -jnp.exp(alog_ref[...])
    dat = dtt * -jnp.exp(alogt_ref[...])

    l_idx = lax.broadcasted_iota(jnp.int32, (CHUNK, CHUNK), 0)
    s_idx = lax.broadcasted_iota(jnp.int32, (CHUNK, CHUNK), 1)
    causal = l_idx >= s_idx
    lower_ones = jnp.where(causal, 1.0, 0.0).astype(F32)
    upper_ones = jnp.where(l_idx <= s_idx, 1.0, 0.0).astype(F32)
    first_head = s_idx < SSM_HEAD_DIM
    first_head_row = first_head[0:1, :]

    for c in range(rows // CHUNK):
        rsl = slice(c * CHUNK, (c + 1) * CHUNK)
        cs = _mm_f32(lower_ones, da[rsl, :])
        cst = _mm_f32(dat[:, rsl], upper_ones)
        cs_last = cs[CHUNK - 1:CHUNK, :]
        dstate = jnp.exp(cs_last - cs)
        ecs = jnp.exp(cs)
        cdec = jnp.exp(cs_last)
        dt_c = dt[rsl, :]

        for g in range(SSM_GROUPS):
            b0 = D_INNER + g * SSM_STATE
            c0 = D_INNER + SSM_GROUPS * SSM_STATE + g * SSM_STATE
            bm = xbc_ref[rsl, b0:b0 + SSM_STATE]
            cm = xbc_ref[rsl, c0:c0 + SSM_STATE].astype(BF16)
            cb = _mm_nt(cm, bm.astype(BF16))
            bmt = bm.T.astype(BF16)

            for r in range(SSM_HPG // 2):
                p = g * (SSM_HPG // 2) + r
                h0, h1 = 2 * p, 2 * p + 1
                cols = slice(p * PAIR_W, (p + 1) * PAIR_W)
                pcols = slice(r * PAIR_W, (r + 1) * PAIR_W)

                def expand(v):
                    return jnp.where(first_head, v[:, h0:h0 + 1], v[:, h1:h1 + 1])

                xs = xbc_ref[rsl, cols]
                top = xs * jnp.where(first_head, dt_c[:, h0:h0 + 1], 0.0)
                bot = xs * jnp.where(first_head, 0.0, dt_c[:, h1:h1 + 1])
                rhs = jnp.concatenate([top.astype(BF16), bot.astype(BF16)], axis=0)

                seg0 = cs[:, h0:h0 + 1] - cst[h0:h0 + 1, :]
                seg1 = cs[:, h1:h1 + 1] - cst[h1:h1 + 1, :]
                m0 = (cb * jnp.exp(jnp.where(causal, seg0, -jnp.inf))).astype(BF16)
                m1 = (cb * jnp.exp(jnp.where(causal, seg1, -jnp.inf))).astype(BF16)
                y_diag = _mm(jnp.concatenate([m0, m1], axis=1), rhs)

                st = state_ref[g, :, pcols]
                y_off = _mm(cm, st.astype(BF16)) * expand(ecs)
                y_ref[rsl, cols] = y_diag + y_off + xs * dexp_ref[:, cols]

                dec = jnp.where(first_head_row, cdec[:, h0:h0 + 1], cdec[:, h1:h1 + 1])
                xds = ((top + bot) * expand(dstate)).astype(BF16)
                state_ref[g, :, pcols] = st * dec + _mm(bmt, xds)

    gw = D_INNER // SSM_GROUPS
    for g in range(SSM_GROUPS):
        cols = slice(g * gw, (g + 1) * gw)
        yg = y_ref[:, cols] * jax.nn.silu(_mm(a, wz_ref[:, cols]))
        yg = yg * lax.rsqrt(jnp.mean(yg * yg, axis=-1, keepdims=True) + EPS)
        o_ref[:, cols] = (yg * gn_ref[:, cols]).astype(o_ref.dtype)

    q = _mm(a, wq_ref[...])
    _mem_attention(q, kt_ref, v_ref, o_ref, D_INNER)


def _mix_b(h, g, w_in, conv_w, conv_b, dt_bias, a_log, d_skip, gnorm, kt, vv):
    bn, s, _ = h.shape
    tm = SCAN_TILE
    z1 = D_INNER
    x1 = z1 + CONV_DIM
    d1 = x1 + SSM_HEADS
    wz, wx = w_in[:, :z1].astype(BF16), w_in[:, z1:x1].astype(BF16)
    wdt, wq = w_in[:, x1:d1].astype(BF16), w_in[:, d1:].astype(BF16)
    return pl.pallas_call(
        _mix_b_kernel,
        grid=(bn, s // tm),
        in_specs=[
            pl.BlockSpec((None, tm, D_MODEL), lambda b, i: (b, i, 0)),
            _resident((1, D_MODEL)),
            _resident((D_MODEL, D_INNER)),
            _resident((D_MODEL, CONV_DIM)),
            _resident((D_MODEL, SSM_HEADS)),
            _resident((SSM_HEADS, D_MODEL)),
            _resident((D_MODEL, X_WIDTH)),
            _resident((CONV_K, CONV_DIM)),
            _resident((1, CONV_DIM)),
            _resident((1, SSM_HEADS)),
            _resident((SSM_HEADS, 1)),
            _resident((1, SSM_HEADS)),
            _resident((SSM_HEADS, 1)),
            _resident((1, D_INNER)),
            _resident((1, D_INNER)),
            pl.BlockSpec((None, X_WIDTH, N_MEM), lambda b, i: (b, 0, 0)),
            pl.BlockSpec((None, N_MEM, X_WIDTH), lambda b, i: (b, 0, 0)),
        ],
        out_specs=pl.BlockSpec((None, tm, MIX_OUT), lambda b, i: (b, i, 0)),
        out_shape=jax.ShapeDtypeStruct((bn, s, MIX_OUT), BF16),
        scratch_shapes=[
            pltpu.VMEM((SSM_GROUPS, SSM_STATE, SSM_GROUP_W), F32),
            pltpu.VMEM((SUBLANES + tm, CONV_DIM), F32),
            pltpu.VMEM((tm, CONV_DIM), F32),
            pltpu.VMEM((tm, D_INNER), F32),
        ],
        compiler_params=pltpu.CompilerParams(
            dimension_semantics=("arbitrary", "arbitrary"), vmem_limit_bytes=VMEM_LIMIT),
        name="mix_b",
    )(h, g.reshape(1, D_MODEL), wz, wx, wdt, wdt.T, wq, conv_w, conv_b.reshape(1, CONV_DIM),
      dt_bias.reshape(1, SSM_HEADS), dt_bias.reshape(SSM_HEADS, 1),
      a_log.reshape(1, SSM_HEADS), a_log.reshape(SSM_HEADS, 1),
      jnp.repeat(d_skip, SSM_HEAD_DIM).reshape(1, D_INNER), gnorm.reshape(1, D_INNER), kt, vv)


def _post_kernel(h_ref, mc_ref, wo_ref, g_ref, w1_ref, w2_ref, fg_ref, o_ref, *, final):
    h = h_ref[...] + _mm(mc_ref[...], wo_ref[...])
    f = _rms(h, g_ref[...]).astype(BF16)
    for j in range(D_FF // FF_COLS):
        cols = slice(j * FF_COLS, (j + 1) * FF_COLS)
        t = jnp.square(jnp.maximum(_mm(f, w1_ref[:, cols]), 0.0)).astype(BF16)
        h = h + _mm(t, w2_ref[cols, :])
    if final:
        h = _rms(h, fg_ref[...])
    o_ref[...] = h


def _post(h, mixcat, w_out, g, w1, w2, final_g, final):
    t = h.shape[0]
    tm = TOKEN_TILE
    return pl.pallas_call(
        functools.partial(_post_kernel, final=final),
        grid=(t // tm,),
        in_specs=[
            pl.BlockSpec((tm, D_MODEL), lambda i: (i, 0)),
            pl.BlockSpec((tm, MIX_OUT), lambda i: (i, 0)),
            _resident((MIX_OUT, D_MODEL)),
            _resident((1, D_MODEL)),
            _resident((D_MODEL, D_FF)),
            _resident((D_FF, D_MODEL)),
            _resident((1, D_MODEL)),
        ],
        out_specs=pl.BlockSpec((tm, D_MODEL), lambda i: (i, 0)),
        out_shape=jax.ShapeDtypeStruct((t, D_MODEL), F32),
        compiler_params=pltpu.CompilerParams(
            dimension_semantics=("parallel",), vmem_limit_bytes=VMEM_LIMIT),
        name="post_final" if final else "post",
    )(h, mixcat, w_out.astype(BF16), g.reshape(1, D_MODEL), w1.astype(BF16), w2.astype(BF16),
      final_g.reshape(1, D_MODEL))


def kernel(x, mem, norm_mix, norm_ffn, mem_norm, w_kv, w_out, w_ffn1, w_ffn2,
           a_in, a_ln_g, a_ln_b, a_ws, a_bs,
           b_in, b_conv_w, b_conv_b, b_dt_bias, b_a_log, b_d, b_gnorm,
           final_norm):
    bn, s, d = x.shape
    depth = w_out.shape[0]
    assert d == D_MODEL and s % TOKEN_TILE == 0 and s % SCAN_TILE == 0
    assert TOKEN_TILE % CHUNK == 0 and SCAN_TILE % CHUNK == 0
    kt, vv = _memory_kv(mem, mem_norm, w_kv)
    h = x.reshape(bn * s, d)
    for i in range(depth):
        j = i // N_MIXERS
        if i % N_MIXERS == 0:
            mixcat = _mix_a(h, bn, norm_mix[i], a_in[j], a_ln_g[j], a_ln_b[j], a_ws[j], a_bs[j],
                            kt[i], vv[i])
        else:
            mixcat = _mix_b(h.reshape(bn, s, d), norm_mix[i], b_in[j], b_conv_w[j], b_conv_b[j],
                            b_dt_bias[j], b_a_log[j], b_d[j], b_gnorm[j], kt[i], vv[i])
            mixcat = mixcat.reshape(bn * s, MIX_OUT)
        h = _post(h, mixcat, w_out[i], norm_ffn[i], w_ffn1[i], w_ffn2[i], final_norm,
                  final=(i == depth - 1))
    return h.reshape(bn, s, d)
```

```python
import functools
import math

import jax
import jax.numpy as jnp
from jax import lax
from jax.experimental import pallas as pl
from jax.experimental.pallas import tpu as pltpu

F32 = jnp.float32
BF16 = jnp.bfloat16

D_MODEL = 1024
N_MIXERS = 2
CHUNK = 128
N_MEM = 256
D_INNER = 2 * D_MODEL
A_GROUPS = 8
A_GROUP_W = D_INNER // A_GROUPS
SSM_HEAD_DIM = 64
SSM_HEADS = D_INNER // SSM_HEAD_DIM
SSM_GROUPS = 4
SSM_HPG = SSM_HEADS // SSM_GROUPS
SSM_STATE = 128
SSM_GROUP_W = SSM_HPG * SSM_HEAD_DIM
CONV_K = 4
CONV_DIM = D_INNER + 2 * SSM_GROUPS * SSM_STATE
X_HEADS = 4
X_HEAD_DIM = 256
X_WIDTH = X_HEADS * X_HEAD_DIM
MIX_OUT = D_INNER + X_WIDTH
D_FF = 4 * D_MODEL
EPS = 1e-6

LANES = 128
SUBLANES = 8
PAIR_W = 2 * SSM_HEAD_DIM
assert PAIR_W == LANES and SSM_STATE == LANES and CHUNK == LANES

TOKEN_TILE = 512
SCAN_TILE = 256
FF_COLS = 1024
CONV_COLS = 256
VMEM_LIMIT = 56 * 1024 * 1024


def _mm(a, b):
    return jnp.dot(a, b, preferred_element_type=F32)


def _mm_nt(a, b):
    return lax.dot_general(a, b, (((1,), (1,)), ((), ())), preferred_element_type=F32)


def _mm_f32(a, b):
    return jnp.dot(a, b, preferred_element_type=F32, precision=lax.Precision.HIGHEST)


def _rms(x, g):
    return x * lax.rsqrt(jnp.mean(x * x, axis=-1, keepdims=True) + EPS) * g


def _gelu(x):
    return 0.5 * x * (1.0 + lax.erf(x * math.sqrt(0.5)))


def _softplus(x):
    return jnp.maximum(x, 0.0) + jnp.log1p(jnp.exp(-jnp.abs(x)))


def _resident(shape):
    zeros = (0,) * len(shape)
    return pl.BlockSpec(shape, lambda *_: zeros, pipeline_mode=pl.Buffered(1))


def _attend_head(q, kt_ref, v_ref, hh):
    sl = slice(hh * X_HEAD_DIM, (hh + 1) * X_HEAD_DIM)
    sc = _mm(q, kt_ref[sl, :])
    e = jnp.exp(sc - jnp.max(sc, axis=-1, keepdims=True))
    return _mm(e.astype(BF16), v_ref[:, sl]) / jnp.sum(e, axis=-1, keepdims=True)


def _mem_attention(q, kt_ref, v_ref, o_ref, col0):
    qb = q.astype(BF16)
    for hh in range(X_HEADS):
        sl = slice(hh * X_HEAD_DIM, (hh + 1) * X_HEAD_DIM)
        o = _attend_head(qb[:, sl], kt_ref, v_ref, hh)
        o_ref[:, col0 + hh * X_HEAD_DIM:col0 + (hh + 1) * X_HEAD_DIM] = o.astype(o_ref.dtype)


def _kv_kernel(mem_ref, g_ref, w_ref, kt_ref, v_ref):
    m = _rms(mem_ref[...], g_ref[...]).astype(BF16)
    kv = _mm(m, w_ref[...])
    scale = 1.0 / math.sqrt(X_HEAD_DIM)
    kt_ref[...] = (kv[:, :X_WIDTH] * scale).T.astype(BF16)
    v_ref[...] = kv[:, X_WIDTH:].astype(BF16)


def _memory_kv(mem, mem_norm, w_kv):
    depth = w_kv.shape[0]
    bn = mem.shape[0]
    return pl.pallas_call(
        _kv_kernel,
        grid=(depth, bn),
        in_specs=[
            pl.BlockSpec((None, N_MEM, D_MODEL), lambda l, b: (b, 0, 0)),
            pl.BlockSpec((None, 1, D_MODEL), lambda l, b: (l, 0, 0)),
            pl.BlockSpec((None, D_MODEL, 2 * X_WIDTH), lambda l, b: (l, 0, 0)),
        ],
        out_specs=[
            pl.BlockSpec((None, None, X_WIDTH, N_MEM), lambda l, b: (l, b, 0, 0)),
            pl.BlockSpec((None, None, N_MEM, X_WIDTH), lambda l, b: (l, b, 0, 0)),
        ],
        out_shape=[
            jax.ShapeDtypeStruct((depth, bn, X_WIDTH, N_MEM), BF16),
            jax.ShapeDtypeStruct((depth, bn, N_MEM, X_WIDTH), BF16),
        ],
        compiler_params=pltpu.CompilerParams(
            dimension_semantics=("parallel", "parallel"), vmem_limit_bytes=VMEM_LIMIT),
        name="kv",
    )(mem, mem_norm.reshape(depth, 1, D_MODEL), w_kv.astype(BF16))


def _mix_a_kernel(h_ref, g_ref, w_ref, lng_ref, lnb_ref, ws_ref, bst_ref, kt_ref, v_ref, o_ref):
    rows = h_ref.shape[0]
    a = _rms(h_ref[...], g_ref[...]).astype(BF16)
    u = _gelu(_mm(a, w_ref[:, 0:D_INNER]))
    v = _gelu(_mm(a, w_ref[:, D_INNER:2 * D_INNER]))
    vc = v - jnp.mean(v, axis=-1, keepdims=True)
    vn = vc * lax.rsqrt(jnp.mean(vc * vc, axis=-1, keepdims=True) + EPS)
    vn = (vn * lng_ref[...] + lnb_ref[...]).astype(BF16)

    t_idx = lax.broadcasted_iota(jnp.int32, (CHUNK, CHUNK), 0)
    s_idx = lax.broadcasted_iota(jnp.int32, (CHUNK, CHUNK), 1)
    causal = t_idx >= s_idx
    for g in range(A_GROUPS):
        cols = slice(g * A_GROUP_W, (g + 1) * A_GROUP_W)
        wg = jnp.where(causal, ws_ref[g], 0.0).astype(BF16)
        bias = bst_ref[:, g:g + 1]
        for c in range(rows // CHUNK):
            rsl = slice(c * CHUNK, (c + 1) * CHUNK)
            sv = _mm(wg, vn[rsl, cols]) + bias
            o_ref[rsl, cols] = (u[rsl, cols] * sv).astype(o_ref.dtype)

    q = _mm(a, w_ref[:, 2 * D_INNER:2 * D_INNER + X_WIDTH])
    _mem_attention(q, kt_ref, v_ref, o_ref, D_INNER)


def _mix_a(h, bn, g, w_in, ln_g, ln_b, ws, bs, kt, vv):
    t = h.shape[0]
    tm = TOKEN_TILE
    tiles_per_batch = (t // bn) // tm
    return pl.pallas_call(
        _mix_a_kernel,
        grid=(t // tm,),
        in_specs=[
            pl.BlockSpec((tm, D_MODEL), lambda i: (i, 0)),
            _resident((1, D_MODEL)),
            _resident((D_MODEL, 2 * D_INNER + X_WIDTH)),
            _resident((1, D_INNER)),
            _resident((1, D_INNER)),
            _resident((A_GROUPS, CHUNK, CHUNK)),
            _resident((CHUNK, A_GROUPS)),
            pl.BlockSpec((None, X_WIDTH, N_MEM), lambda i: (i // tiles_per_batch, 0, 0)),
            pl.BlockSpec((None, N_MEM, X_WIDTH), lambda i: (i // tiles_per_batch, 0, 0)),
        ],
        out_specs=pl.BlockSpec((tm, MIX_OUT), lambda i: (i, 0)),
        out_shape=jax.ShapeDtypeStruct((t, MIX_OUT), BF16),
        compiler_params=pltpu.CompilerParams(
            dimension_semantics=("parallel",), vmem_limit_bytes=VMEM_LIMIT),
        name="mix_a",
    )(h, g.reshape(1, D_MODEL), w_in.astype(BF16), ln_g.reshape(1, D_INNER), ln_b.reshape(1, D_INNER),
      ws, bs.T, kt, vv)


def _mix_b_kernel(h_ref, g_ref, wz_ref, wx_ref, wdt_ref, wdtt_ref, wq_ref, cw_ref, cb_ref,
                  dtb_ref, dtbt_ref, alog_ref, alogt_ref, dexp_ref, gn_ref, kt_ref, v_ref,
                  o_ref, state_ref, ext_ref, xbc_ref, y_ref):
    rows = h_ref.shape[0]
    n_chunks = rows // CHUNK
    pairs_per_group = SSM_HPG // 2
    x_blocks_per_group = SSM_GROUP_W // CONV_COLS

    @pl.when(pl.program_id(1) == 0)
    def _():
        state_ref[...] = jnp.zeros_like(state_ref)
        ext_ref[0:SUBLANES, :] = jnp.zeros((SUBLANES, CONV_DIM), F32)

    a = _rms(h_ref[...], g_ref[...]).astype(BF16)

    def conv_block(j):
        cols = slice(j * CONV_COLS, (j + 1) * CONV_COLS)
        ext_ref[SUBLANES:SUBLANES + rows, cols] = _mm(a, wx_ref[:, cols])
        x = ext_ref[:, cols]
        acc = x[SUBLANES:] * cw_ref[CONV_K - 1:CONV_K, cols]
        for s in range(1, CONV_K):
            acc = acc + pltpu.roll(x, s, 0)[SUBLANES:] * cw_ref[CONV_K - 1 - s:CONV_K - s, cols]
        xbc_ref[:, cols] = jax.nn.silu(acc + cb_ref[:, cols])
        ext_ref[0:SUBLANES, cols] = x[rows:]

    for j in range(D_INNER // CONV_COLS, CONV_DIM // CONV_COLS):
        conv_block(j)
    for j in range(x_blocks_per_group):
        conv_block(j)

    dt = _softplus(_mm(a, wdt_ref[...]) + dtb_ref[...])
    dtt = _softplus(_mm_nt(wdtt_ref[...], a) + dtbt_ref[...])
    da = dt * -jnp.exp(alog_ref[...])
    dat = dtt * -jnp.exp(alogt_ref[...])

    l_idx = lax.broadcasted_iota(jnp.int32, (CHUNK, CHUNK), 0)
    s_idx = lax.broadcasted_iota(jnp.int32, (CHUNK, CHUNK), 1)
    causal = l_idx >= s_idx
    lower_ones = jnp.where(causal, 1.0, 0.0).astype(F32)
    upper_ones = jnp.where(l_idx <= s_idx, 1.0, 0.0).astype(F32)
    first_head = s_idx < SSM_HEAD_DIM
    first_head_row = first_head[0:1, :]

    chunk_vals = []
    for c in range(n_chunks):
        rsl = slice(c * CHUNK, (c + 1) * CHUNK)
        cs = _mm_f32(lower_ones, da[rsl, :])
        cst = _mm_f32(dat[:, rsl], upper_ones)
        dtt_c = dtt[:, rsl]
        chunk_vals.append(dict(
            rsl=rsl, cs=cs, cst=cst, dtt=dtt_c,
            cdec=jnp.exp(cs[CHUNK - 1:CHUNK, :]),
            wst=jnp.exp(cst[:, CHUNK - 1:CHUNK] - cst) * dtt_c))

    def group_operands(c, g):
        rsl = chunk_vals[c]["rsl"]
        b0 = D_INNER + g * SSM_STATE
        c0 = D_INNER + SSM_GROUPS * SSM_STATE + g * SSM_STATE
        bm = xbc_ref[rsl, b0:b0 + SSM_STATE]
        cm = xbc_ref[rsl, c0:c0 + SSM_STATE].astype(BF16)
        return dict(cm=cm, cb=_mm_nt(cm, bm.astype(BF16)),
                    bmt=bm.T)

    def ssd_pair(c, g, r, gv):
        cv = chunk_vals[c]
        rsl, cs, cst = cv["rsl"], cv["cs"], cv["cst"]
        p = g * pairs_per_group + r
        h0, h1 = 2 * p, 2 * p + 1
        cols = slice(p * PAIR_W, (p + 1) * PAIR_W)
        pcols = slice(r * PAIR_W, (r + 1) * PAIR_W)

        xs = xbc_ref[rsl, cols]
        rhs = jnp.concatenate([jnp.where(first_head, xs, 0.0).astype(BF16),
                               jnp.where(first_head, 0.0, xs).astype(BF16)], axis=0)

        csb0 = jnp.broadcast_to(cs[:, h0:h0 + 1], (CHUNK, CHUNK))
        csb1 = jnp.broadcast_to(cs[:, h1:h1 + 1], (CHUNK, CHUNK))

        def scores(csb, h):
            decay = jnp.exp(jnp.where(causal, csb - cst[h:h + 1, :], -jnp.inf))
            return (gv["cb"] * decay * cv["dtt"][h:h + 1, :]).astype(BF16)

        y_diag = _mm(jnp.concatenate([scores(csb0, h0), scores(csb1, h1)], axis=1), rhs)

        st = state_ref[g, :, pcols]
        y_off = _mm(gv["cm"], st.astype(BF16)) * jnp.where(first_head, jnp.exp(csb0), jnp.exp(csb1))
        y_ref[rsl, cols] = y_diag + y_off + xs * dexp_ref[:, cols]

        dec = jnp.where(first_head_row, cv["cdec"][:, h0:h0 + 1], cv["cdec"][:, h1:h1 + 1])
        bts = jnp.concatenate([(gv["bmt"] * cv["wst"][h0:h0 + 1, :]).astype(BF16),
                               (gv["bmt"] * cv["wst"][h1:h1 + 1, :]).astype(BF16)], axis=1)
        state_ref[g, :, pcols] = st * dec + _mm(bts, rhs)

    z_parts = {}

    def z_block(j):
        cols = slice(j * CONV_COLS, (j + 1) * CONV_COLS)
        z_parts[j] = jax.nn.silu(_mm(a, wz_ref[:, cols]))

    def gate(g):
        cols = slice(g * SSM_GROUP_W, (g + 1) * SSM_GROUP_W)
        zs = jnp.concatenate([z_parts.pop(g * x_blocks_per_group + k) for k in range(x_blocks_per_group)],
                             axis=1)
        yg = y_ref[:, cols] * zs
        yg = yg * lax.rsqrt(jnp.mean(yg * yg, axis=-1, keepdims=True) + EPS)
        o_ref[:, cols] = (yg * gn_ref[:, cols]).astype(o_ref.dtype)

    def attend(hh):
        sl = slice(hh * X_HEAD_DIM, (hh + 1) * X_HEAD_DIM)
        o = _attend_head(_mm(a, wq_ref[:, sl]).astype(BF16), kt_ref, v_ref, hh)
        o_ref[:, D_INNER + hh * X_HEAD_DIM:D_INNER + (hh + 1) * X_HEAD_DIM] = o.astype(o_ref.dtype)

    assert SSM_GROUPS == X_HEADS
    group_vals = {}

    def prepare_group(g):
        group_vals[g] = [group_operands(c, g) for c in range(n_chunks)]

    prepare_group(0)
    for g in range(SSM_GROUPS):
        extras = []
        if g + 1 < SSM_GROUPS:
            extras += [functools.partial(conv_block, (g + 1) * x_blocks_per_group + k)
                       for k in range(x_blocks_per_group)]
            extras.append(functools.partial(prepare_group, g + 1))
        if g > 0:
            extras.append(functools.partial(gate, g - 1))
        extras += [functools.partial(z_block, g * x_blocks_per_group + k) for k in range(x_blocks_per_group)]
        extras.append(functools.partial(attend, g))
        for c in range(n_chunks):
            for r in range(pairs_per_group):
                ssd_pair(c, g, r, group_vals[g][c])
                if extras:
                    extras.pop(0)()
        while extras:
            extras.pop(0)()
    gate(SSM_GROUPS - 1)


def _mix_b(h, bn, g, w_in, conv_w, conv_b, dt_bias, a_log, d_skip, gnorm, kt, vv):
    t = h.shape[0]
    tm = SCAN_TILE
    tiles_per_batch = (t // bn) // tm
    z1 = D_INNER
    x1 = z1 + CONV_DIM
    d1 = x1 + SSM_HEADS
    wz, wx = w_in[:, :z1].astype(BF16), w_in[:, z1:x1].astype(BF16)
    wdt, wq = w_in[:, x1:d1].astype(BF16), w_in[:, d1:].astype(BF16)
    return pl.pallas_call(
        _mix_b_kernel,
        grid=(bn, tiles_per_batch),
        in_specs=[
            pl.BlockSpec((tm, D_MODEL), lambda b, i: (b * tiles_per_batch + i, 0)),
            _resident((1, D_MODEL)),
            _resident((D_MODEL, D_INNER)),
            _resident((D_MODEL, CONV_DIM)),
            _resident((D_MODEL, SSM_HEADS)),
            _resident((SSM_HEADS, D_MODEL)),
            _resident((D_MODEL, X_WIDTH)),
            _resident((CONV_K, CONV_DIM)),
            _resident((1, CONV_DIM)),
            _resident((1, SSM_HEADS)),
            _resident((SSM_HEADS, 1)),
            _resident((1, SSM_HEADS)),
            _resident((SSM_HEADS, 1)),
            _resident((1, D_INNER)),
            _resident((1, D_INNER)),
            pl.BlockSpec((None, X_WIDTH, N_MEM), lambda b, i: (b, 0, 0)),
            pl.BlockSpec((None, N_MEM, X_WIDTH), lambda b, i: (b, 0, 0)),
        ],
        out_specs=pl.BlockSpec((tm, MIX_OUT), lambda b, i: (b * tiles_per_batch + i, 0)),
        out_shape=jax.ShapeDtypeStruct((t, MIX_OUT), BF16),
        scratch_shapes=[
            pltpu.VMEM((SSM_GROUPS, SSM_STATE, SSM_GROUP_W), F32),
            pltpu.VMEM((SUBLANES + tm, CONV_DIM), F32),
            pltpu.VMEM((tm, CONV_DIM), F32),
            pltpu.VMEM((tm, D_INNER), F32),
        ],
        compiler_params=pltpu.CompilerParams(
            dimension_semantics=("arbitrary", "arbitrary"), vmem_limit_bytes=VMEM_LIMIT),
        name="mix_b",
    )(h, g.reshape(1, D_MODEL), wz, wx, wdt, wdt.T, wq, conv_w, conv_b.reshape(1, CONV_DIM),
      dt_bias.reshape(1, SSM_HEADS), dt_bias.reshape(SSM_HEADS, 1),
      a_log.reshape(1, SSM_HEADS), a_log.reshape(SSM_HEADS, 1),
      jnp.repeat(d_skip, SSM_HEAD_DIM).reshape(1, D_INNER), gnorm.reshape(1, D_INNER), kt, vv)


def _post_kernel(h_ref, mc_ref, wo_ref, g_ref, w1_ref, w2_ref, fg_ref, o_ref, *, final):
    h = h_ref[...] + _mm(mc_ref[...], wo_ref[...])
    f = _rms(h, g_ref[...]).astype(BF16)
    for j in range(D_FF // FF_COLS):
        cols = slice(j * FF_COLS, (j + 1) * FF_COLS)
        t = jnp.square(jnp.maximum(_mm(f, w1_ref[:, cols]), 0.0)).astype(BF16)
        h = h + _mm(t, w2_ref[cols, :])
    if final:
        h = _rms(h, fg_ref[...])
    o_ref[...] = h


def _post(h, mixcat, w_out, g, w1, w2, final_g, final):
    t = h.shape[0]
    tm = TOKEN_TILE
    return pl.pallas_call(
        functools.partial(_post_kernel, final=final),
        grid=(t // tm,),
        in_specs=[
            pl.BlockSpec((tm, D_MODEL), lambda i: (i, 0)),
            pl.BlockSpec((tm, MIX_OUT), lambda i: (i, 0)),
            _resident((MIX_OUT, D_MODEL)),
            _resident((1, D_MODEL)),
            _resident((D_MODEL, D_FF)),
            _resident((D_FF, D_MODEL)),
            _resident((1, D_MODEL)),
        ],
        out_specs=pl.BlockSpec((tm, D_MODEL), lambda i: (i, 0)),
        out_shape=jax.ShapeDtypeStruct((t, D_MODEL), F32),
        compiler_params=pltpu.CompilerParams(
            dimension_semantics=("parallel",), vmem_limit_bytes=VMEM_LIMIT),
        name="post_final" if final else "post",
    )(h, mixcat, w_out.astype(BF16), g.reshape(1, D_MODEL), w1.astype(BF16), w2.astype(BF16),
      final_g.reshape(1, D_MODEL))


def kernel(x, mem, norm_mix, norm_ffn, mem_norm, w_kv, w_out, w_ffn1, w_ffn2,
           a_in, a_ln_g, a_ln_b, a_ws, a_bs,
           b_in, b_conv_w, b_conv_b, b_dt_bias, b_a_log, b_d, b_gnorm,
           final_norm):
    bn, s, d = x.shape
    depth = w_out.shape[0]
    assert d == D_MODEL and s % TOKEN_TILE == 0 and s % SCAN_TILE == 0
    assert TOKEN_TILE % CHUNK == 0 and SCAN_TILE % CHUNK == 0
    kt, vv = _memory_kv(mem, mem_norm, w_kv)
    h = x.reshape(bn * s, d)
    for i in range(depth):
        j = i // N_MIXERS
        if i % N_MIXERS == 0:
            mixcat = _mix_a(h, bn, norm_mix[i], a_in[j], a_ln_g[j], a_ln_b[j], a_ws[j], a_bs[j],
                            kt[i], vv[i])
        else:
            mixcat = _mix_b(h, bn, norm_mix[i], b_in[j], b_conv_w[j], b_conv_b[j],
                            b_dt_bias[j], b_a_log[j], b_d[j], b_gnorm[j], kt[i], vv[i])
        h = _post(h, mixcat, w_out[i], norm_ffn[i], w_ffn1[i], w_ffn2[i], final_norm,
                  final=(i == depth - 1))
    return h.reshape(bn, s, d)
```

```python
import functools
import math

import jax
import jax.numpy as jnp
from jax import lax
from jax.experimental import pallas as pl
from jax.experimental.pallas import tpu as pltpu

F32 = jnp.float32
BF16 = jnp.bfloat16

D_MODEL = 1024
N_MIXERS = 2
CHUNK = 128
N_MEM = 256
D_INNER = 2 * D_MODEL
A_GROUPS = 8
A_GROUP_W = D_INNER // A_GROUPS
SSM_HEAD_DIM = 64
SSM_HEADS = D_INNER // SSM_HEAD_DIM
SSM_GROUPS = 4
SSM_HPG = SSM_HEADS // SSM_GROUPS
SSM_STATE = 128
SSM_GROUP_W = SSM_HPG * SSM_HEAD_DIM
CONV_K = 4
CONV_DIM = D_INNER + 2 * SSM_GROUPS * SSM_STATE
X_HEADS = 4
X_HEAD_DIM = 256
X_WIDTH = X_HEADS * X_HEAD_DIM
MIX_OUT = D_INNER + X_WIDTH
D_FF = 4 * D_MODEL
EPS = 1e-6

LANES = 128
SUBLANES = 8
PAIR_W = 2 * SSM_HEAD_DIM
assert PAIR_W == LANES and SSM_STATE == LANES and CHUNK == LANES

TOKEN_TILE = 512
SCAN_TILE = 256
FF_COLS = 1024
MLP_ITEM_COLS = 512
CONV_COLS = 256
VMEM_LIMIT = 56 * 1024 * 1024


def _mm(a, b):
    return jnp.dot(a, b, preferred_element_type=F32)


def _mm_nt(a, b):
    return lax.dot_general(a, b, (((1,), (1,)), ((), ())), preferred_element_type=F32)


def _mm_f32(a, b):
    return jnp.dot(a, b, preferred_element_type=F32, precision=lax.Precision.HIGHEST)


def _rms(x, g):
    return x * lax.rsqrt(jnp.mean(x * x, axis=-1, keepdims=True) + EPS) * g


def _gelu(x):
    return 0.5 * x * (1.0 + lax.erf(x * math.sqrt(0.5)))


def _softplus(x):
    return jnp.maximum(x, 0.0) + jnp.log1p(jnp.exp(-jnp.abs(x)))


def _resident(shape):
    zeros = (0,) * len(shape)
    return pl.BlockSpec(shape, lambda *_: zeros, pipeline_mode=pl.Buffered(1))


def _attend_head(q, kt_ref, v_ref, hh):
    sl = slice(hh * X_HEAD_DIM, (hh + 1) * X_HEAD_DIM)
    sc = _mm(q, kt_ref[sl, :])
    e = jnp.exp(sc - jnp.max(sc, axis=-1, keepdims=True))
    return _mm(e.astype(BF16), v_ref[:, sl]) / jnp.sum(e, axis=-1, keepdims=True)


def _mem_attention(q, kt_ref, v_ref, o_ref, col0):
    qb = q.astype(BF16)
    for hh in range(X_HEADS):
        sl = slice(hh * X_HEAD_DIM, (hh + 1) * X_HEAD_DIM)
        o = _attend_head(qb[:, sl], kt_ref, v_ref, hh)
        o_ref[:, col0 + hh * X_HEAD_DIM:col0 + (hh + 1) * X_HEAD_DIM] = o.astype(o_ref.dtype)


def _kv_kernel(mem_ref, g_ref, w_ref, kt_ref, v_ref):
    m = _rms(mem_ref[...], g_ref[...]).astype(BF16)
    kv = _mm(m, w_ref[...])
    scale = 1.0 / math.sqrt(X_HEAD_DIM)
    kt_ref[...] = (kv[:, :X_WIDTH] * scale).T.astype(BF16)
    v_ref[...] = kv[:, X_WIDTH:].astype(BF16)


def _memory_kv(mem, mem_norm, w_kv):
    depth = w_kv.shape[0]
    bn = mem.shape[0]
    return pl.pallas_call(
        _kv_kernel,
        grid=(depth, bn),
        in_specs=[
            pl.BlockSpec((None, N_MEM, D_MODEL), lambda l, b: (b, 0, 0)),
            pl.BlockSpec((None, 1, D_MODEL), lambda l, b: (l, 0, 0)),
            pl.BlockSpec((None, D_MODEL, 2 * X_WIDTH), lambda l, b: (l, 0, 0)),
        ],
        out_specs=[
            pl.BlockSpec((None, None, X_WIDTH, N_MEM), lambda l, b: (l, b, 0, 0)),
            pl.BlockSpec((None, None, N_MEM, X_WIDTH), lambda l, b: (l, b, 0, 0)),
        ],
        out_shape=[
            jax.ShapeDtypeStruct((depth, bn, X_WIDTH, N_MEM), BF16),
            jax.ShapeDtypeStruct((depth, bn, N_MEM, X_WIDTH), BF16),
        ],
        compiler_params=pltpu.CompilerParams(
            dimension_semantics=("parallel", "parallel"), vmem_limit_bytes=VMEM_LIMIT),
        name="kv",
    )(mem, mem_norm.reshape(depth, 1, D_MODEL), w_kv.astype(BF16))


def _mix_a_kernel(h_ref, g_ref, w_ref, lng_ref, lnb_ref, ws_ref, bst_ref, kt_ref, v_ref, wo_ref, o_ref,
                  mc_ref):
    rows = h_ref.shape[0]
    a = _rms(h_ref[...], g_ref[...]).astype(BF16)
    u = _gelu(_mm(a, w_ref[:, 0:D_INNER]))
    v = _gelu(_mm(a, w_ref[:, D_INNER:2 * D_INNER]))
    vc = v - jnp.mean(v, axis=-1, keepdims=True)
    vn = vc * lax.rsqrt(jnp.mean(vc * vc, axis=-1, keepdims=True) + EPS)
    vn = (vn * lng_ref[...] + lnb_ref[...]).astype(BF16)

    t_idx = lax.broadcasted_iota(jnp.int32, (CHUNK, CHUNK), 0)
    s_idx = lax.broadcasted_iota(jnp.int32, (CHUNK, CHUNK), 1)
    causal = t_idx >= s_idx
    for g in range(A_GROUPS):
        cols = slice(g * A_GROUP_W, (g + 1) * A_GROUP_W)
        wg = jnp.where(causal, ws_ref[g], 0.0).astype(BF16)
        bias = bst_ref[:, g:g + 1]
        for c in range(rows // CHUNK):
            rsl = slice(c * CHUNK, (c + 1) * CHUNK)
            sv = _mm(wg, vn[rsl, cols]) + bias
            mc_ref[rsl, cols] = (u[rsl, cols] * sv).astype(mc_ref.dtype)

    q = _mm(a, w_ref[:, 2 * D_INNER:2 * D_INNER + X_WIDTH])
    _mem_attention(q, kt_ref, v_ref, mc_ref, D_INNER)
    o_ref[...] = h_ref[...] + _mm(mc_ref[...], wo_ref[...])


def _mix_a(h, bn, g, w_in, ln_g, ln_b, ws, bs, kt, vv, w_out):
    t = h.shape[0]
    tm = TOKEN_TILE
    tiles_per_batch = (t // bn) // tm
    return pl.pallas_call(
        _mix_a_kernel,
        grid=(t // tm,),
        in_specs=[
            pl.BlockSpec((tm, D_MODEL), lambda i: (i, 0)),
            _resident((1, D_MODEL)),
            _resident((D_MODEL, 2 * D_INNER + X_WIDTH)),
            _resident((1, D_INNER)),
            _resident((1, D_INNER)),
            _resident((A_GROUPS, CHUNK, CHUNK)),
            _resident((CHUNK, A_GROUPS)),
            pl.BlockSpec((None, X_WIDTH, N_MEM), lambda i: (i // tiles_per_batch, 0, 0)),
            pl.BlockSpec((None, N_MEM, X_WIDTH), lambda i: (i // tiles_per_batch, 0, 0)),
            _resident((MIX_OUT, D_MODEL)),
        ],
        out_specs=pl.BlockSpec((tm, D_MODEL), lambda i: (i, 0)),
        out_shape=jax.ShapeDtypeStruct((t, D_MODEL), F32),
        scratch_shapes=[pltpu.VMEM((tm, MIX_OUT), BF16)],
        compiler_params=pltpu.CompilerParams(
            dimension_semantics=("parallel",), vmem_limit_bytes=VMEM_LIMIT),
        name="mix_a",
    )(h, g.reshape(1, D_MODEL), w_in.astype(BF16), ln_g.reshape(1, D_INNER), ln_b.reshape(1, D_INNER),
      ws, bs.T, kt, vv, w_out.astype(BF16))


def _ffn_mix_b_kernel(hp_ref, gf_ref, w1_ref, w2_ref,
                      g_ref, wz_ref, wx_ref, wdt_ref, wdtt_ref, wq_ref, cw_ref, cb_ref,
                      dtb_ref, dtbt_ref, alog_ref, alogt_ref, dexp_ref, gn_ref, kt_ref, v_ref,
                      hn_ref, o_ref, hc_ref, state_ref, ext_ref, xbc_ref, y_ref, *, tiles_per_batch):
    rows = hp_ref.shape[0]
    n_chunks = rows // CHUNK
    pairs_per_group = SSM_HPG // 2
    x_blocks_per_group = SSM_GROUP_W // CONV_COLS
    j = pl.program_id(0)

    @pl.when(j == 0)
    def _():
        hc_ref[...] = jnp.zeros_like(hc_ref)

    @pl.when(jnp.logical_or(j == 0, lax.rem(j + tiles_per_batch - 1, tiles_per_batch) == 0))
    def _():
        state_ref[...] = jnp.zeros_like(state_ref)
        ext_ref[0:SUBLANES, :] = jnp.zeros((SUBLANES, CONV_DIM), F32)

    a = _rms(hc_ref[...], g_ref[...]).astype(BF16)

    hp = hp_ref[...]
    f = _rms(hp, gf_ref[...]).astype(BF16)
    mlp_acc = [hp]

    def mlp_block(k):
        cols = slice(k * MLP_ITEM_COLS, (k + 1) * MLP_ITEM_COLS)
        t = jnp.square(jnp.maximum(_mm(f, w1_ref[:, cols]), 0.0)).astype(BF16)
        mlp_acc[0] = mlp_acc[0] + _mm(t, w2_ref[cols, :])

    mlp_items = [functools.partial(mlp_block, k) for k in range(D_FF // MLP_ITEM_COLS)]
    mlp_per_group = len(mlp_items) // SSM_GROUPS

    def conv_block(j):
        cols = slice(j * CONV_COLS, (j + 1) * CONV_COLS)
        ext_ref[SUBLANES:SUBLANES + rows, cols] = _mm(a, wx_ref[:, cols])
        x = ext_ref[:, cols]
        acc = x[SUBLANES:] * cw_ref[CONV_K - 1:CONV_K, cols]
        for s in range(1, CONV_K):
            acc = acc + pltpu.roll(x, s, 0)[SUBLANES:] * cw_ref[CONV_K - 1 - s:CONV_K - s, cols]
        xbc_ref[:, cols] = jax.nn.silu(acc + cb_ref[:, cols])
        ext_ref[0:SUBLANES, cols] = x[rows:]

    for j in range(D_INNER // CONV_COLS, CONV_DIM // CONV_COLS):
        conv_block(j)
    for j in range(x_blocks_per_group):
        conv_block(j)

    dt = _softplus(_mm(a, wdt_ref[...]) + dtb_ref[...])
    dtt = _softplus(_mm_nt(wdtt_ref[...], a) + dtbt_ref[...])
    da = dt * -jnp.exp(alog_ref[...])
    dat = dtt * -jnp.exp(alogt_ref[...])

    l_idx = lax.broadcasted_iota(jnp.int32, (CHUNK, CHUNK), 0)
    s_idx = lax.broadcasted_iota(jnp.int32, (CHUNK, CHUNK), 1)
    causal = l_idx >= s_idx
    lower_ones = jnp.where(causal, 1.0, 0.0).astype(F32)
    upper_ones = jnp.where(l_idx <= s_idx, 1.0, 0.0).astype(F32)
    first_head = s_idx < SSM_HEAD_DIM
    first_head_row = first_head[0:1, :]

    chunk_vals = []
    for c in range(n_chunks):
        rsl = slice(c * CHUNK, (c + 1) * CHUNK)
        cs = _mm_f32(lower_ones, da[rsl, :])
        cst = _mm_f32(dat[:, rsl], upper_ones)
        dtt_c = dtt[:, rsl]
        chunk_vals.append(dict(
            rsl=rsl, cs=cs, cst=cst, dtt=dtt_c,
            cdec=jnp.exp(cs[CHUNK - 1:CHUNK, :]),
            wst=jnp.exp(cst[:, CHUNK - 1:CHUNK] - cst) * dtt_c))

    def group_operands(c, g):
        rsl = chunk_vals[c]["rsl"]
        b0 = D_INNER + g * SSM_STATE
        c0 = D_INNER + SSM_GROUPS * SSM_STATE + g * SSM_STATE
        bm = xbc_ref[rsl, b0:b0 + SSM_STATE]
        cm = xbc_ref[rsl, c0:c0 + SSM_STATE].astype(BF16)
        return dict(cm=cm, cb=_mm_nt(cm, bm.astype(BF16)),
                    bmt=bm.T)

    def ssd_pair(c, g, r, gv):
        cv = chunk_vals[c]
        rsl, cs, cst = cv["rsl"], cv["cs"], cv["cst"]
        p = g * pairs_per_group + r
        h0, h1 = 2 * p, 2 * p + 1
        cols = slice(p * PAIR_W, (p + 1) * PAIR_W)
        pcols = slice(r * PAIR_W, (r + 1) * PAIR_W)

        xs = xbc_ref[rsl, cols]
        rhs = jnp.concatenate([jnp.where(first_head, xs, 0.0).astype(BF16),
                               jnp.where(first_head, 0.0, xs).astype(BF16)], axis=0)

        csb0 = jnp.broadcast_to(cs[:, h0:h0 + 1], (CHUNK, CHUNK))
        csb1 = jnp.broadcast_to(cs[:, h1:h1 + 1], (CHUNK, CHUNK))

        def scores(csb, h):
            decay = jnp.exp(jnp.where(causal, csb - cst[h:h + 1, :], -jnp.inf))
            return (gv["cb"] * decay * cv["dtt"][h:h + 1, :]).astype(BF16)

        y_diag = _mm(jnp.concatenate([scores(csb0, h0), scores(csb1, h1)], axis=1), rhs)

        st = state_ref[g, :, pcols]
        y_off = _mm(gv["cm"], st.astype(BF16)) * jnp.where(first_head, jnp.exp(csb0), jnp.exp(csb1))
        y_ref[rsl, cols] = y_diag + y_off + xs * dexp_ref[:, cols]

        dec = jnp.where(first_head_row, cv["cdec"][:, h0:h0 + 1], cv["cdec"][:, h1:h1 + 1])
        bts = jnp.concatenate([(gv["bmt"] * cv["wst"][h0:h0 + 1, :]).astype(BF16),
                               (gv["bmt"] * cv["wst"][h1:h1 + 1, :]).astype(BF16)], axis=1)
        state_ref[g, :, pcols] = st * dec + _mm(bts, rhs)

    z_parts = {}

    def z_block(j):
        cols = slice(j * CONV_COLS, (j + 1) * CONV_COLS)
        z_parts[j] = jax.nn.silu(_mm(a, wz_ref[:, cols]))

    def gate(g):
        cols = slice(g * SSM_GROUP_W, (g + 1) * SSM_GROUP_W)
        zs = jnp.concatenate([z_parts.pop(g * x_blocks_per_group + k) for k in range(x_blocks_per_group)],
                             axis=1)
        yg = y_ref[:, cols] * zs
        yg = yg * lax.rsqrt(jnp.mean(yg * yg, axis=-1, keepdims=True) + EPS)
        o_ref[:, cols] = (yg * gn_ref[:, cols]).astype(o_ref.dtype)

    def attend(hh):
        sl = slice(hh * X_HEAD_DIM, (hh + 1) * X_HEAD_DIM)
        o = _attend_head(_mm(a, wq_ref[:, sl]).astype(BF16), kt_ref, v_ref, hh)
        o_ref[:, D_INNER + hh * X_HEAD_DIM:D_INNER + (hh + 1) * X_HEAD_DIM] = o.astype(o_ref.dtype)

    assert SSM_GROUPS == X_HEADS
    group_vals = {}

    def prepare_group(g):
        group_vals[g] = [group_operands(c, g) for c in range(n_chunks)]

    prepare_group(0)
    for g in range(SSM_GROUPS):
        extras = []
        if g + 1 < SSM_GROUPS:
            extras += [functools.partial(conv_block, (g + 1) * x_blocks_per_group + k)
                       for k in range(x_blocks_per_group)]
            extras.append(functools.partial(prepare_group, g + 1))
        if g > 0:
            extras.append(functools.partial(gate, g - 1))
        extras += [functools.partial(z_block, g * x_blocks_per_group + k) for k in range(x_blocks_per_group)]
        extras.append(functools.partial(attend, g))
        mlp = mlp_items[g * mlp_per_group:(g + 1) * mlp_per_group]
        stride = max(1, len(extras) // (len(mlp) + 1))
        for k, item in enumerate(mlp):
            extras.insert(min(len(extras), (k + 1) * stride + k), item)
        n_pairs = n_chunks * pairs_per_group
        n_extras = len(extras)
        done = 0
        for c in range(n_chunks):
            for r in range(pairs_per_group):
                ssd_pair(c, g, r, group_vals[g][c])
                done += 1
                while len(extras) > n_extras - (done * n_extras) // n_pairs:
                    extras.pop(0)()
    gate(SSM_GROUPS - 1)
    hn_ref[...] = mlp_acc[0]
    hc_ref[...] = mlp_acc[0]


def _ffn_mix_b(hp, bn, gf, w1, w2, g, w_in, conv_w, conv_b, dt_bias, a_log, d_skip, gnorm, kt, vv):
    t = hp.shape[0]
    tm = SCAN_TILE
    n_tiles = t // tm
    tiles_per_batch = n_tiles // bn
    z1 = D_INNER
    x1 = z1 + CONV_DIM
    d1 = x1 + SSM_HEADS
    wz, wx = w_in[:, :z1].astype(BF16), w_in[:, z1:x1].astype(BF16)
    wdt, wq = w_in[:, x1:d1].astype(BF16), w_in[:, d1:].astype(BF16)

    def mlp_tile(j):
        return jnp.minimum(j, n_tiles - 1)

    def mix_tile(j):
        return jnp.maximum(j - 1, 0)

    return pl.pallas_call(
        functools.partial(_ffn_mix_b_kernel, tiles_per_batch=tiles_per_batch),
        grid=(n_tiles + 1,),
        in_specs=[
            pl.BlockSpec((tm, D_MODEL), lambda j: (mlp_tile(j), 0)),
            _resident((1, D_MODEL)),
            _resident((D_MODEL, D_FF)),
            _resident((D_FF, D_MODEL)),
            _resident((1, D_MODEL)),
            _resident((D_MODEL, D_INNER)),
            _resident((D_MODEL, CONV_DIM)),
            _resident((D_MODEL, SSM_HEADS)),
            _resident((SSM_HEADS, D_MODEL)),
            _resident((D_MODEL, X_WIDTH)),
            _resident((CONV_K, CONV_DIM)),
            _resident((1, CONV_DIM)),
            _resident((1, SSM_HEADS)),
            _resident((SSM_HEADS, 1)),
            _resident((1, SSM_HEADS)),
            _resident((SSM_HEADS, 1)),
            _resident((1, D_INNER)),
            _resident((1, D_INNER)),
            pl.BlockSpec((None, X_WIDTH, N_MEM), lambda j: (mix_tile(j) // tiles_per_batch, 0, 0)),
            pl.BlockSpec((None, N_MEM, X_WIDTH), lambda j: (mix_tile(j) // tiles_per_batch, 0, 0)),
        ],
        out_specs=[
            pl.BlockSpec((tm, D_MODEL), lambda j: (mlp_tile(j), 0)),
            pl.BlockSpec((tm, MIX_OUT), lambda j: (mix_tile(j), 0)),
        ],
        out_shape=[
            jax.ShapeDtypeStruct((t, D_MODEL), F32),
            jax.ShapeDtypeStruct((t, MIX_OUT), BF16),
        ],
        scratch_shapes=[
            pltpu.VMEM((tm, D_MODEL), F32),
            pltpu.VMEM((SSM_GROUPS, SSM_STATE, SSM_GROUP_W), F32),
            pltpu.VMEM((SUBLANES + tm, CONV_DIM), F32),
            pltpu.VMEM((tm, CONV_DIM), F32),
            pltpu.VMEM((tm, D_INNER), F32),
        ],
        compiler_params=pltpu.CompilerParams(
            dimension_semantics=("arbitrary",), vmem_limit_bytes=VMEM_LIMIT),
        name="ffn_mix_b",
    )(hp, gf.reshape(1, D_MODEL), w1.astype(BF16), w2.astype(BF16),
      g.reshape(1, D_MODEL), wz, wx, wdt, wdt.T, wq, conv_w, conv_b.reshape(1, CONV_DIM),
      dt_bias.reshape(1, SSM_HEADS), dt_bias.reshape(SSM_HEADS, 1),
      a_log.reshape(1, SSM_HEADS), a_log.reshape(SSM_HEADS, 1),
      jnp.repeat(d_skip, SSM_HEAD_DIM).reshape(1, D_INNER), gnorm.reshape(1, D_INNER), kt, vv)


def _post_kernel(h_ref, mc_ref, wo_ref, g_ref, w1_ref, w2_ref, fg_ref, o_ref, *, final):
    h = h_ref[...] + _mm(mc_ref[...], wo_ref[...])
    f = _rms(h, g_ref[...]).astype(BF16)
    for j in range(D_FF // FF_COLS):
        cols = slice(j * FF_COLS, (j + 1) * FF_COLS)
        t = jnp.square(jnp.maximum(_mm(f, w1_ref[:, cols]), 0.0)).astype(BF16)
        h = h + _mm(t, w2_ref[cols, :])
    if final:
        h = _rms(h, fg_ref[...])
    o_ref[...] = h


def _post(h, mixcat, w_out, g, w1, w2, final_g, final):
    t = h.shape[0]
    tm = TOKEN_TILE
    return pl.pallas_call(
        functools.partial(_post_kernel, final=final),
        grid=(t // tm,),
        in_specs=[
            pl.BlockSpec((tm, D_MODEL), lambda i: (i, 0)),
            pl.BlockSpec((tm, MIX_OUT), lambda i: (i, 0)),
            _resident((MIX_OUT, D_MODEL)),
            _resident((1, D_MODEL)),
            _resident((D_MODEL, D_FF)),
            _resident((D_FF, D_MODEL)),
            _resident((1, D_MODEL)),
        ],
        out_specs=pl.BlockSpec((tm, D_MODEL), lambda i: (i, 0)),
        out_shape=jax.ShapeDtypeStruct((t, D_MODEL), F32),
        compiler_params=pltpu.CompilerParams(
            dimension_semantics=("parallel",), vmem_limit_bytes=VMEM_LIMIT),
        name="post_final" if final else "post",
    )(h, mixcat, w_out.astype(BF16), g.reshape(1, D_MODEL), w1.astype(BF16), w2.astype(BF16),
      final_g.reshape(1, D_MODEL))


def kernel(x, mem, norm_mix, norm_ffn, mem_norm, w_kv, w_out, w_ffn1, w_ffn2,
           a_in, a_ln_g, a_ln_b, a_ws, a_bs,
           b_in, b_conv_w, b_conv_b, b_dt_bias, b_a_log, b_d, b_gnorm,
           final_norm):
    bn, s, d = x.shape
    depth = w_out.shape[0]
    assert depth == 2 and N_MIXERS == 2
    assert d == D_MODEL and s % TOKEN_TILE == 0 and s % SCAN_TILE == 0
    assert TOKEN_TILE % CHUNK == 0 and SCAN_TILE % CHUNK == 0
    kt, vv = _memory_kv(mem, mem_norm, w_kv)
    h = x.reshape(bn * s, d)
    h = _mix_a(h, bn, norm_mix[0], a_in[0], a_ln_g[0], a_ln_b[0], a_ws[0], a_bs[0], kt[0], vv[0], w_out[0])
    h, mixcat = _ffn_mix_b(h, bn, norm_ffn[0], w_ffn1[0], w_ffn2[0], norm_mix[1], b_in[0], b_conv_w[0],
                           b_conv_b[0], b_dt_bias[0], b_a_log[0], b_d[0], b_gnorm[0], kt[1], vv[1])
    h = _post(h, mixcat, w_out[1], norm_ffn[1], w_ffn1[1], w_ffn2[1], final_norm, final=True)
    return h.reshape(bn, s, d)
```

```python
import functools
import math

import jax
import jax.numpy as jnp
from jax import lax
from jax.experimental import pallas as pl
from jax.experimental.pallas import tpu as pltpu

F32 = jnp.float32
BF16 = jnp.bfloat16

D_MODEL = 1024
N_MIXERS = 2
CHUNK = 128
N_MEM = 256
D_INNER = 2 * D_MODEL
A_GROUPS = 8
A_GROUP_W = D_INNER // A_GROUPS
SSM_HEAD_DIM = 64
SSM_HEADS = D_INNER // SSM_HEAD_DIM
SSM_GROUPS = 4
SSM_HPG = SSM_HEADS // SSM_GROUPS
SSM_STATE = 128
SSM_GROUP_W = SSM_HPG * SSM_HEAD_DIM
CONV_K = 4
CONV_DIM = D_INNER + 2 * SSM_GROUPS * SSM_STATE
X_HEADS = 4
X_HEAD_DIM = 256
X_WIDTH = X_HEADS * X_HEAD_DIM
MIX_OUT = D_INNER + X_WIDTH
D_FF = 4 * D_MODEL
EPS = 1e-6

LANES = 128
SUBLANES = 8
PAIR_W = 2 * SSM_HEAD_DIM
assert PAIR_W == LANES and SSM_STATE == LANES and CHUNK == LANES

TOKEN_TILE = 512
SCAN_TILE = 256
FF_COLS = 1024
CONV_COLS = 256
VMEM_LIMIT = 56 * 1024 * 1024
STAGE_BYTES = 1024 * 1024


def _mm(a, b):
    return jnp.dot(a, b, preferred_element_type=F32)


def _mm_nt(a, b):
    return lax.dot_general(a, b, (((1,), (1,)), ((), ())), preferred_element_type=F32)


def _mm_f32(a, b):
    return jnp.dot(a, b, preferred_element_type=F32, precision=lax.Precision.HIGHEST)


def _rms(x, g):
    return x * lax.rsqrt(jnp.mean(x * x, axis=-1, keepdims=True) + EPS) * g


def _gelu(x):
    return 0.5 * x * (1.0 + lax.erf(x * math.sqrt(0.5)))


def _softplus(x):
    return jnp.maximum(x, 0.0) + jnp.log1p(jnp.exp(-jnp.abs(x)))


def _resident(shape):
    zeros = (0,) * len(shape)
    return pl.BlockSpec(shape, lambda *_: zeros, pipeline_mode=pl.Buffered(1))


def _stage_weight(w_hbm, layer, w_vmem, stage, sem):
    rows, cols = w_vmem.shape
    chunk = stage.shape[1]
    assert stage.shape[2] == cols and rows % chunk == 0
    n = rows // chunk

    def copy(k):
        return pltpu.make_async_copy(w_hbm.at[layer, pl.ds(k * chunk, chunk), pl.ds(0, cols)],
                                     stage.at[k % 2], sem.at[k % 2])

    copy(0).start()
    for k in range(n):
        if k + 1 < n:
            copy(k + 1).start()
        copy(k).wait()
        w_vmem[k * chunk:(k + 1) * chunk, :] = stage[k % 2].astype(BF16)


def _stage_shape(rows, cols):
    chunk = 8
    while chunk * 2 * cols * 4 <= STAGE_BYTES and rows % (chunk * 2) == 0:
        chunk *= 2
    return (2, chunk, cols)


def _hbm():
    return pl.BlockSpec(memory_space=pl.ANY)


def _attend_head(q, kt_ref, v_ref, hh):
    sl = slice(hh * X_HEAD_DIM, (hh + 1) * X_HEAD_DIM)
    sc = _mm(q, kt_ref[sl, :])
    e = jnp.exp(sc - jnp.max(sc, axis=-1, keepdims=True))
    return _mm(e.astype(BF16), v_ref[:, sl]) / jnp.sum(e, axis=-1, keepdims=True)


def _mem_attention(q, kt_ref, v_ref, o_ref, col0):
    qb = q.astype(BF16)
    for hh in range(X_HEADS):
        sl = slice(hh * X_HEAD_DIM, (hh + 1) * X_HEAD_DIM)
        o = _attend_head(qb[:, sl], kt_ref, v_ref, hh)
        o_ref[:, col0 + hh * X_HEAD_DIM:col0 + (hh + 1) * X_HEAD_DIM] = o.astype(o_ref.dtype)


def _kv_kernel(mem_ref, g_ref, w_ref, kt_ref, v_ref):
    m = _rms(mem_ref[...], g_ref[...]).astype(BF16)
    kv = _mm(m, w_ref[...].astype(BF16))
    scale = 1.0 / math.sqrt(X_HEAD_DIM)
    kt_ref[...] = (kv[:, :X_WIDTH] * scale).T.astype(BF16)
    v_ref[...] = kv[:, X_WIDTH:].astype(BF16)


def _memory_kv(mem, mem_norm, w_kv):
    depth = w_kv.shape[0]
    bn = mem.shape[0]
    return pl.pallas_call(
        _kv_kernel,
        grid=(depth, bn),
        in_specs=[
            pl.BlockSpec((None, N_MEM, D_MODEL), lambda l, b: (b, 0, 0)),
            pl.BlockSpec((None, 1, D_MODEL), lambda l, b: (l, 0, 0)),
            pl.BlockSpec((None, D_MODEL, 2 * X_WIDTH), lambda l, b: (l, 0, 0)),
        ],
        out_specs=[
            pl.BlockSpec((None, None, X_WIDTH, N_MEM), lambda l, b: (l, b, 0, 0)),
            pl.BlockSpec((None, None, N_MEM, X_WIDTH), lambda l, b: (l, b, 0, 0)),
        ],
        out_shape=[
            jax.ShapeDtypeStruct((depth, bn, X_WIDTH, N_MEM), BF16),
            jax.ShapeDtypeStruct((depth, bn, N_MEM, X_WIDTH), BF16),
        ],
        compiler_params=pltpu.CompilerParams(
            dimension_semantics=("parallel", "parallel"), vmem_limit_bytes=VMEM_LIMIT),
        name="kv",
    )(mem, mem_norm.reshape(depth, 1, D_MODEL), w_kv)


def _mix_a_kernel(h_ref, g_ref, w_hbm, lng_ref, lnb_ref, ws_ref, bst_ref, kt_ref, v_ref, o_ref,
                  w_ref, stage, sem, *, layer):
    rows = h_ref.shape[0]

    @pl.when(pl.program_id(0) == 0)
    def _():
        _stage_weight(w_hbm, layer, w_ref, stage, sem)

    a = _rms(h_ref[...], g_ref[...]).astype(BF16)
    u = _gelu(_mm(a, w_ref[:, 0:D_INNER]))
    v = _gelu(_mm(a, w_ref[:, D_INNER:2 * D_INNER]))
    vc = v - jnp.mean(v, axis=-1, keepdims=True)
    vn = vc * lax.rsqrt(jnp.mean(vc * vc, axis=-1, keepdims=True) + EPS)
    vn = (vn * lng_ref[...] + lnb_ref[...]).astype(BF16)

    t_idx = lax.broadcasted_iota(jnp.int32, (CHUNK, CHUNK), 0)
    s_idx = lax.broadcasted_iota(jnp.int32, (CHUNK, CHUNK), 1)
    causal = t_idx >= s_idx
    for g in range(A_GROUPS):
        cols = slice(g * A_GROUP_W, (g + 1) * A_GROUP_W)
        wg = jnp.where(causal, ws_ref[g], 0.0).astype(BF16)
        bias = bst_ref[:, g:g + 1]
        for c in range(rows // CHUNK):
            rsl = slice(c * CHUNK, (c + 1) * CHUNK)
            sv = _mm(wg, vn[rsl, cols]) + bias
            o_ref[rsl, cols] = (u[rsl, cols] * sv).astype(o_ref.dtype)

    q = _mm(a, w_ref[:, 2 * D_INNER:2 * D_INNER + X_WIDTH])
    _mem_attention(q, kt_ref, v_ref, o_ref, D_INNER)


def _mix_a(h, bn, g, w_in, layer, ln_g, ln_b, ws, bs, kt, vv):
    t = h.shape[0]
    tm = TOKEN_TILE
    tiles_per_batch = (t // bn) // tm
    a_in = 2 * D_INNER + X_WIDTH
    return pl.pallas_call(
        functools.partial(_mix_a_kernel, layer=layer),
        grid=(t // tm,),
        in_specs=[
            pl.BlockSpec((tm, D_MODEL), lambda i: (i, 0)),
            _resident((1, D_MODEL)),
            _hbm(),
            _resident((1, D_INNER)),
            _resident((1, D_INNER)),
            _resident((A_GROUPS, CHUNK, CHUNK)),
            _resident((CHUNK, A_GROUPS)),
            pl.BlockSpec((None, X_WIDTH, N_MEM), lambda i: (i // tiles_per_batch, 0, 0)),
            pl.BlockSpec((None, N_MEM, X_WIDTH), lambda i: (i // tiles_per_batch, 0, 0)),
        ],
        out_specs=pl.BlockSpec((tm, MIX_OUT), lambda i: (i, 0)),
        out_shape=jax.ShapeDtypeStruct((t, MIX_OUT), BF16),
        scratch_shapes=[
            pltpu.VMEM((D_MODEL, a_in), BF16),
            pltpu.VMEM(_stage_shape(D_MODEL, a_in), F32),
            pltpu.SemaphoreType.DMA((2,)),
        ],
        compiler_params=pltpu.CompilerParams(
            dimension_semantics=("arbitrary",), vmem_limit_bytes=VMEM_LIMIT),
        name="mix_a",
    )(h, g.reshape(1, D_MODEL), w_in, ln_g.reshape(1, D_INNER), ln_b.reshape(1, D_INNER),
      ws, bs.T, kt, vv)


def _mix_b_kernel(h_ref, g_ref, w_hbm, wdt_ref, wdtt_ref, wq_ref, cw_ref, cb_ref,
                  dtb_ref, dtbt_ref, alog_ref, alogt_ref, dexp_ref, gn_ref, kt_ref, v_ref,
                  o_ref, state_ref, ext_ref, xbc_ref, y_ref, wzx_ref, stage, sem, *, layer):
    rows = h_ref.shape[0]
    n_chunks = rows // CHUNK
    pairs_per_group = SSM_HPG // 2
    x_blocks_per_group = SSM_GROUP_W // CONV_COLS

    @pl.when(jnp.logical_and(pl.program_id(0) == 0, pl.program_id(1) == 0))
    def _():
        _stage_weight(w_hbm, layer, wzx_ref, stage, sem)

    @pl.when(pl.program_id(1) == 0)
    def _():
        state_ref[...] = jnp.zeros_like(state_ref)
        ext_ref[0:SUBLANES, :] = jnp.zeros((SUBLANES, CONV_DIM), F32)

    a = _rms(h_ref[...], g_ref[...]).astype(BF16)

    def conv_block(j):
        cols = slice(j * CONV_COLS, (j + 1) * CONV_COLS)
        ext_ref[SUBLANES:SUBLANES + rows, cols] = _mm(
            a, wzx_ref[:, D_INNER + j * CONV_COLS:D_INNER + (j + 1) * CONV_COLS])
        x = ext_ref[:, cols]
        acc = x[SUBLANES:] * cw_ref[CONV_K - 1:CONV_K, cols]
        for s in range(1, CONV_K):
            acc = acc + pltpu.roll(x, s, 0)[SUBLANES:] * cw_ref[CONV_K - 1 - s:CONV_K - s, cols]
        xbc_ref[:, cols] = jax.nn.silu(acc + cb_ref[:, cols])
        ext_ref[0:SUBLANES, cols] = x[rows:]

    for j in range(D_INNER // CONV_COLS, CONV_DIM // CONV_COLS):
        conv_block(j)
    for j in range(x_blocks_per_group):
        conv_block(j)

    dt = _softplus(_mm(a, wdt_ref[...]) + dtb_ref[...])
    dtt = _softplus(_mm_nt(wdtt_ref[...], a) + dtbt_ref[...])
    da = dt * -jnp.exp(alog_ref[...])
    dat = dtt * -jnp.exp(alogt_ref[...])

    l_idx = lax.broadcasted_iota(jnp.int32, (CHUNK, CHUNK), 0)
    s_idx = lax.broadcasted_iota(jnp.int32, (CHUNK, CHUNK), 1)
    causal = l_idx >= s_idx
    lower_ones = jnp.where(causal, 1.0, 0.0).astype(F32)
    upper_ones = jnp.where(l_idx <= s_idx, 1.0, 0.0).astype(F32)
    first_head = s_idx < SSM_HEAD_DIM
    first_head_row = first_head[0:1, :]

    chunk_vals = []
    for c in range(n_chunks):
        rsl = slice(c * CHUNK, (c + 1) * CHUNK)
        cs = _mm_f32(lower_ones, da[rsl, :])
        cst = _mm_f32(dat[:, rsl], upper_ones)
        dtt_c = dtt[:, rsl]
        chunk_vals.append(dict(
            rsl=rsl, cs=cs, cst=cst, dtt=dtt_c,
            cdec=jnp.exp(cs[CHUNK - 1:CHUNK, :]),
            wst=jnp.exp(cst[:, CHUNK - 1:CHUNK] - cst) * dtt_c))

    def group_operands(c, g):
        rsl = chunk_vals[c]["rsl"]
        b0 = D_INNER + g * SSM_STATE
        c0 = D_INNER + SSM_GROUPS * SSM_STATE + g * SSM_STATE
        bm = xbc_ref[rsl, b0:b0 + SSM_STATE]
        cm = xbc_ref[rsl, c0:c0 + SSM_STATE].astype(BF16)
        return dict(cm=cm, cb=_mm_nt(cm, bm.astype(BF16)),
                    bmt=bm.T)

    def ssd_pair(c, g, r, gv):
        cv = chunk_vals[c]
        rsl, cs, cst = cv["rsl"], cv["cs"], cv["cst"]
        p = g * pairs_per_group + r
        h0, h1 = 2 * p, 2 * p + 1
        cols = slice(p * PAIR_W, (p + 1) * PAIR_W)
        pcols = slice(r * PAIR_W, (r + 1) * PAIR_W)

        xs = xbc_ref[rsl, cols]
        rhs = jnp.concatenate([jnp.where(first_head, xs, 0.0).astype(BF16),
                               jnp.where(first_head, 0.0, xs).astype(BF16)], axis=0)

        csb0 = jnp.broadcast_to(cs[:, h0:h0 + 1], (CHUNK, CHUNK))
        csb1 = jnp.broadcast_to(cs[:, h1:h1 + 1], (CHUNK, CHUNK))

        def scores(csb, h):
            decay = jnp.exp(jnp.where(causal, csb - cst[h:h + 1, :], -jnp.inf))
            return (gv["cb"] * decay * cv["dtt"][h:h + 1, :]).astype(BF16)

        y_diag = _mm(jnp.concatenate([scores(csb0, h0), scores(csb1, h1)], axis=1), rhs)

        st = state_ref[g, :, pcols]
        y_off = _mm(gv["cm"], st.astype(BF16)) * jnp.where(first_head, jnp.exp(csb0), jnp.exp(csb1))
        y_ref[rsl, cols] = y_diag + y_off + xs * dexp_ref[:, cols]

        dec = jnp.where(first_head_row, cv["cdec"][:, h0:h0 + 1], cv["cdec"][:, h1:h1 + 1])
        bts = jnp.concatenate([(gv["bmt"] * cv["wst"][h0:h0 + 1, :]).astype(BF16),
                               (gv["bmt"] * cv["wst"][h1:h1 + 1, :]).astype(BF16)], axis=1)
        state_ref[g, :, pcols] = st * dec + _mm(bts, rhs)

    z_parts = {}

    def z_block(j):
        cols = slice(j * CONV_COLS, (j + 1) * CONV_COLS)
        z_parts[j] = jax.nn.silu(_mm(a, wzx_ref[:, cols]))

    def gate(g):
        cols = slice(g * SSM_GROUP_W, (g + 1) * SSM_GROUP_W)
        zs = jnp.concatenate([z_parts.pop(g * x_blocks_per_group + k) for k in range(x_blocks_per_group)],
                             axis=1)
        yg = y_ref[:, cols] * zs
        yg = yg * lax.rsqrt(jnp.mean(yg * yg, axis=-1, keepdims=True) + EPS)
        o_ref[:, cols] = (yg * gn_ref[:, cols]).astype(o_ref.dtype)

    def attend(hh):
        sl = slice(hh * X_HEAD_DIM, (hh + 1) * X_HEAD_DIM)
        o = _attend_head(_mm(a, wq_ref[:, sl]).astype(BF16), kt_ref, v_ref, hh)
        o_ref[:, D_INNER + hh * X_HEAD_DIM:D_INNER + (hh + 1) * X_HEAD_DIM] = o.astype(o_ref.dtype)

    assert SSM_GROUPS == X_HEADS
    group_vals = {}

    def prepare_group(g):
        group_vals[g] = [group_operands(c, g) for c in range(n_chunks)]

    prepare_group(0)
    for g in range(SSM_GROUPS):
        extras = []
        if g + 1 < SSM_GROUPS:
            extras += [functools.partial(conv_block, (g + 1) * x_blocks_per_group + k)
                       for k in range(x_blocks_per_group)]
            extras.append(functools.partial(prepare_group, g + 1))
        if g > 0:
            extras.append(functools.partial(gate, g - 1))
        extras += [functools.partial(z_block, g * x_blocks_per_group + k) for k in range(x_blocks_per_group)]
        extras.append(functools.partial(attend, g))
        for c in range(n_chunks):
            for r in range(pairs_per_group):
                ssd_pair(c, g, r, group_vals[g][c])
                if extras:
                    extras.pop(0)()
        while extras:
            extras.pop(0)()
    gate(SSM_GROUPS - 1)


def _mix_b(h, bn, g, w_in, layer, conv_w, conv_b, dt_bias, a_log, d_skip, gnorm, kt, vv):
    t = h.shape[0]
    tm = SCAN_TILE
    tiles_per_batch = (t // bn) // tm
    x1 = D_INNER + CONV_DIM
    d1 = x1 + SSM_HEADS
    wdt, wq = w_in[layer, :, x1:d1].astype(BF16), w_in[layer, :, d1:].astype(BF16)
    return pl.pallas_call(
        functools.partial(_mix_b_kernel, layer=layer),
        grid=(bn, tiles_per_batch),
        in_specs=[
            pl.BlockSpec((tm, D_MODEL), lambda b, i: (b * tiles_per_batch + i, 0)),
            _resident((1, D_MODEL)),
            _hbm(),
            _resident((D_MODEL, SSM_HEADS)),
            _resident((SSM_HEADS, D_MODEL)),
            _resident((D_MODEL, X_WIDTH)),
            _resident((CONV_K, CONV_DIM)),
            _resident((1, CONV_DIM)),
            _resident((1, SSM_HEADS)),
            _resident((SSM_HEADS, 1)),
            _resident((1, SSM_HEADS)),
            _resident((SSM_HEADS, 1)),
            _resident((1, D_INNER)),
            _resident((1, D_INNER)),
            pl.BlockSpec((None, X_WIDTH, N_MEM), lambda b, i: (b, 0, 0)),
            pl.BlockSpec((None, N_MEM, X_WIDTH), lambda b, i: (b, 0, 0)),
        ],
        out_specs=pl.BlockSpec((tm, MIX_OUT), lambda b, i: (b * tiles_per_batch + i, 0)),
        out_shape=jax.ShapeDtypeStruct((t, MIX_OUT), BF16),
        scratch_shapes=[
            pltpu.VMEM((SSM_GROUPS, SSM_STATE, SSM_GROUP_W), F32),
            pltpu.VMEM((SUBLANES + tm, CONV_DIM), F32),
            pltpu.VMEM((tm, CONV_DIM), F32),
            pltpu.VMEM((tm, D_INNER), F32),
            pltpu.VMEM((D_MODEL, x1), BF16),
            pltpu.VMEM(_stage_shape(D_MODEL, x1), F32),
            pltpu.SemaphoreType.DMA((2,)),
        ],
        compiler_params=pltpu.CompilerParams(
            dimension_semantics=("arbitrary", "arbitrary"), vmem_limit_bytes=VMEM_LIMIT),
        name="mix_b",
    )(h, g.reshape(1, D_MODEL), w_in, wdt, wdt.T, wq, conv_w, conv_b.reshape(1, CONV_DIM),
      dt_bias.reshape(1, SSM_HEADS), dt_bias.reshape(SSM_HEADS, 1),
      a_log.reshape(1, SSM_HEADS), a_log.reshape(SSM_HEADS, 1),
      jnp.repeat(d_skip, SSM_HEAD_DIM).reshape(1, D_INNER), gnorm.reshape(1, D_INNER), kt, vv)


def _post_kernel(h_ref, mc_ref, wo_hbm, g_ref, w1_hbm, w2_hbm, fg_ref, o_ref,
                 wo_ref, w1_ref, w2_ref, stage_d, stage_ff, sem, *, layer, final):
    @pl.when(pl.program_id(0) == 0)
    def _():
        _stage_weight(wo_hbm, layer, wo_ref, stage_d, sem)
        _stage_weight(w1_hbm, layer, w1_ref, stage_ff, sem)
        _stage_weight(w2_hbm, layer, w2_ref, stage_d, sem)

    h = h_ref[...] + _mm(mc_ref[...], wo_ref[...])
    f = _rms(h, g_ref[...]).astype(BF16)
    for j in range(D_FF // FF_COLS):
        cols = slice(j * FF_COLS, (j + 1) * FF_COLS)
        t = jnp.square(jnp.maximum(_mm(f, w1_ref[:, cols]), 0.0)).astype(BF16)
        h = h + _mm(t, w2_ref[cols, :])
    if final:
        h = _rms(h, fg_ref[...])
    o_ref[...] = h


def _post(h, mixcat, w_out, g, w1, w2, final_g, layer, final):
    t = h.shape[0]
    tm = TOKEN_TILE
    return pl.pallas_call(
        functools.partial(_post_kernel, layer=layer, final=final),
        grid=(t // tm,),
        in_specs=[
            pl.BlockSpec((tm, D_MODEL), lambda i: (i, 0)),
            pl.BlockSpec((tm, MIX_OUT), lambda i: (i, 0)),
            _hbm(),
            _resident((1, D_MODEL)),
            _hbm(),
            _hbm(),
            _resident((1, D_MODEL)),
        ],
        out_specs=pl.BlockSpec((tm, D_MODEL), lambda i: (i, 0)),
        out_shape=jax.ShapeDtypeStruct((t, D_MODEL), F32),
        scratch_shapes=[
            pltpu.VMEM((MIX_OUT, D_MODEL), BF16),
            pltpu.VMEM((D_MODEL, D_FF), BF16),
            pltpu.VMEM((D_FF, D_MODEL), BF16),
            pltpu.VMEM(_stage_shape(MIX_OUT, D_MODEL), F32),
            pltpu.VMEM(_stage_shape(D_MODEL, D_FF), F32),
            pltpu.SemaphoreType.DMA((2,)),
        ],
        compiler_params=pltpu.CompilerParams(
            dimension_semantics=("arbitrary",), vmem_limit_bytes=VMEM_LIMIT),
        name="post_final" if final else "post",
    )(h, mixcat, w_out, g.reshape(1, D_MODEL), w1, w2, final_g.reshape(1, D_MODEL))


def kernel(x, mem, norm_mix, norm_ffn, mem_norm, w_kv, w_out, w_ffn1, w_ffn2,
           a_in, a_ln_g, a_ln_b, a_ws, a_bs,
           b_in, b_conv_w, b_conv_b, b_dt_bias, b_a_log, b_d, b_gnorm,
           final_norm):
    bn, s, d = x.shape
    depth = w_out.shape[0]
    assert d == D_MODEL and s % TOKEN_TILE == 0 and s % SCAN_TILE == 0
    assert TOKEN_TILE % CHUNK == 0 and SCAN_TILE % CHUNK == 0
    kt, vv = _memory_kv(mem, mem_norm, w_kv)
    h = x.reshape(bn * s, d)
    for i in range(depth):
        j = i // N_MIXERS
        if i % N_MIXERS == 0:
            mixcat = _mix_a(h, bn, norm_mix[i], a_in, j, a_ln_g[j], a_ln_b[j], a_ws[j], a_bs[j],
                            kt[i], vv[i])
        else:
            mixcat = _mix_b(h, bn, norm_mix[i], b_in, j, b_conv_w[j], b_conv_b[j],
                            b_dt_bias[j], b_a_log[j], b_d[j], b_gnorm[j], kt[i], vv[i])
        h = _post(h, mixcat, w_out, norm_ffn[i], w_ffn1, w_ffn2, final_norm, layer=i,
                  final=(i == depth - 1))
    return h.reshape(bn, s, d)
```

```python
import functools
import math

import jax
import jax.numpy as jnp
from jax import lax
from jax.experimental import pallas as pl
from jax.experimental.pallas import tpu as pltpu

F32 = jnp.float32
BF16 = jnp.bfloat16

D_MODEL = 1024
N_MIXERS = 2
CHUNK = 128
N_MEM = 256
D_INNER = 2 * D_MODEL
A_GROUPS = 8
A_GROUP_W = D_INNER // A_GROUPS
SSM_HEAD_DIM = 64
SSM_HEADS = D_INNER // SSM_HEAD_DIM
SSM_GROUPS = 4
SSM_HPG = SSM_HEADS // SSM_GROUPS
SSM_STATE = 128
SSM_GROUP_W = SSM_HPG * SSM_HEAD_DIM
CONV_K = 4
CONV_DIM = D_INNER + 2 * SSM_GROUPS * SSM_STATE
X_HEADS = 4
X_HEAD_DIM = 256
X_WIDTH = X_HEADS * X_HEAD_DIM
MIX_OUT = D_INNER + X_WIDTH
D_FF = 4 * D_MODEL
EPS = 1e-6

LANES = 128
SUBLANES = 8
PAIR_W = 2 * SSM_HEAD_DIM
assert PAIR_W == LANES and SSM_STATE == LANES and CHUNK == LANES

TOKEN_TILE = 512
SCAN_TILE = 256
FF_COLS = 1024
CONV_COLS = 256
VMEM_LIMIT = 56 * 1024 * 1024


def _mm(a, b):
    return jnp.dot(a, b, preferred_element_type=F32)


def _mm_nt(a, b):
    return lax.dot_general(a, b, (((1,), (1,)), ((), ())), preferred_element_type=F32)


def _mm_f32(a, b):
    return jnp.dot(a, b, preferred_element_type=F32, precision=lax.Precision.HIGHEST)


def _rms(x, g):
    return x * lax.rsqrt(jnp.mean(x * x, axis=-1, keepdims=True) + EPS) * g


def _gelu(x):
    return 0.5 * x * (1.0 + lax.erf(x * math.sqrt(0.5)))


def _softplus(x):
    return jnp.maximum(x, 0.0) + jnp.log1p(jnp.exp(-jnp.abs(x)))


def _resident(shape):
    zeros = (0,) * len(shape)
    return pl.BlockSpec(shape, lambda *_: zeros, pipeline_mode=pl.Buffered(1))


def _cast_specs(weights, n_steps, step_index):
    in_specs, out_specs, out_shapes = [], [], []
    for w, layer, cols in weights:
        rows = w.shape[1] // n_steps
        assert rows * n_steps == w.shape[1] and rows % 16 == 0 and cols % LANES == 0
        in_specs.append(pl.BlockSpec((None, rows, cols), lambda *g, layer=layer: (layer, step_index(*g), 0)))
        out_specs.append(pl.BlockSpec((rows, cols), lambda *g: (step_index(*g), 0)))
        out_shapes.append(jax.ShapeDtypeStruct((w.shape[1], cols), BF16))
    return in_specs, out_specs, out_shapes


def _cast_blocks(refs):
    n = len(refs) // 2
    for src_ref, dst_ref in zip(refs[:n], refs[n:]):
        dst_ref[...] = src_ref[...].astype(BF16)


def _attend_head(q, kt_ref, v_ref, hh):
    sl = slice(hh * X_HEAD_DIM, (hh + 1) * X_HEAD_DIM)
    sc = _mm(q, kt_ref[sl, :])
    e = jnp.exp(sc - jnp.max(sc, axis=-1, keepdims=True))
    return _mm(e.astype(BF16), v_ref[:, sl]) / jnp.sum(e, axis=-1, keepdims=True)


def _mem_attention(q, kt_ref, v_ref, o_ref, col0):
    qb = q.astype(BF16)
    for hh in range(X_HEADS):
        sl = slice(hh * X_HEAD_DIM, (hh + 1) * X_HEAD_DIM)
        o = _attend_head(qb[:, sl], kt_ref, v_ref, hh)
        o_ref[:, col0 + hh * X_HEAD_DIM:col0 + (hh + 1) * X_HEAD_DIM] = o.astype(o_ref.dtype)


def _kv_kernel(mem_ref, g_ref, w_ref, *refs):
    n_cast = (len(refs) - 2) // 2
    kt_ref, v_ref = refs[n_cast:n_cast + 2]
    _cast_blocks(refs[:n_cast] + refs[n_cast + 2:])
    m = _rms(mem_ref[...], g_ref[...]).astype(BF16)
    kv = _mm(m, w_ref[...].astype(BF16))
    scale = 1.0 / math.sqrt(X_HEAD_DIM)
    kt_ref[...] = (kv[:, :X_WIDTH] * scale).T.astype(BF16)
    v_ref[...] = kv[:, X_WIDTH:].astype(BF16)


def _memory_kv(mem, mem_norm, w_kv, cast):
    depth = w_kv.shape[0]
    bn = mem.shape[0]
    c_in, c_out, c_shapes = _cast_specs(cast, depth * bn, lambda l, b: l * bn + b)
    return pl.pallas_call(
        _kv_kernel,
        grid=(depth, bn),
        in_specs=[
            pl.BlockSpec((None, N_MEM, D_MODEL), lambda l, b: (b, 0, 0)),
            pl.BlockSpec((None, 1, D_MODEL), lambda l, b: (l, 0, 0)),
            pl.BlockSpec((None, D_MODEL, 2 * X_WIDTH), lambda l, b: (l, 0, 0)),
        ] + c_in,
        out_specs=[
            pl.BlockSpec((None, None, X_WIDTH, N_MEM), lambda l, b: (l, b, 0, 0)),
            pl.BlockSpec((None, None, N_MEM, X_WIDTH), lambda l, b: (l, b, 0, 0)),
        ] + c_out,
        out_shape=[
            jax.ShapeDtypeStruct((depth, bn, X_WIDTH, N_MEM), BF16),
            jax.ShapeDtypeStruct((depth, bn, N_MEM, X_WIDTH), BF16),
        ] + c_shapes,
        compiler_params=pltpu.CompilerParams(
            dimension_semantics=("parallel", "parallel"), vmem_limit_bytes=VMEM_LIMIT),
        name="kv",
    )(mem, mem_norm.reshape(depth, 1, D_MODEL), w_kv, *[w for w, _, _ in cast])


def _mix_a_kernel(h_ref, g_ref, w_ref, lng_ref, lnb_ref, ws_ref, bst_ref, kt_ref, v_ref, *refs):
    n_cast = (len(refs) - 1) // 2
    o_ref = refs[n_cast]
    _cast_blocks(refs[:n_cast] + refs[n_cast + 1:])
    rows = h_ref.shape[0]
    a = _rms(h_ref[...], g_ref[...]).astype(BF16)
    u = _gelu(_mm(a, w_ref[:, 0:D_INNER]))
    v = _gelu(_mm(a, w_ref[:, D_INNER:2 * D_INNER]))
    vc = v - jnp.mean(v, axis=-1, keepdims=True)
    vn = vc * lax.rsqrt(jnp.mean(vc * vc, axis=-1, keepdims=True) + EPS)
    vn = (vn * lng_ref[...] + lnb_ref[...]).astype(BF16)

    t_idx = lax.broadcasted_iota(jnp.int32, (CHUNK, CHUNK), 0)
    s_idx = lax.broadcasted_iota(jnp.int32, (CHUNK, CHUNK), 1)
    causal = t_idx >= s_idx
    for g in range(A_GROUPS):
        cols = slice(g * A_GROUP_W, (g + 1) * A_GROUP_W)
        wg = jnp.where(causal, ws_ref[g], 0.0).astype(BF16)
        bias = bst_ref[:, g:g + 1]
        for c in range(rows // CHUNK):
            rsl = slice(c * CHUNK, (c + 1) * CHUNK)
            sv = _mm(wg, vn[rsl, cols]) + bias
            o_ref[rsl, cols] = (u[rsl, cols] * sv).astype(o_ref.dtype)

    q = _mm(a, w_ref[:, 2 * D_INNER:2 * D_INNER + X_WIDTH])
    _mem_attention(q, kt_ref, v_ref, o_ref, D_INNER)


def _mix_a(h, bn, g, w_in, ln_g, ln_b, ws, bs, kt, vv, cast):
    t = h.shape[0]
    tm = TOKEN_TILE
    tiles_per_batch = (t // bn) // tm
    c_in, c_out, c_shapes = _cast_specs(cast, t // tm, lambda i: i)
    return pl.pallas_call(
        _mix_a_kernel,
        grid=(t // tm,),
        in_specs=[
            pl.BlockSpec((tm, D_MODEL), lambda i: (i, 0)),
            _resident((1, D_MODEL)),
            _resident((D_MODEL, 2 * D_INNER + X_WIDTH)),
            _resident((1, D_INNER)),
            _resident((1, D_INNER)),
            _resident((A_GROUPS, CHUNK, CHUNK)),
            _resident((CHUNK, A_GROUPS)),
            pl.BlockSpec((None, X_WIDTH, N_MEM), lambda i: (i // tiles_per_batch, 0, 0)),
            pl.BlockSpec((None, N_MEM, X_WIDTH), lambda i: (i // tiles_per_batch, 0, 0)),
        ] + c_in,
        out_specs=[pl.BlockSpec((tm, MIX_OUT), lambda i: (i, 0))] + c_out,
        out_shape=[jax.ShapeDtypeStruct((t, MIX_OUT), BF16)] + c_shapes,
        compiler_params=pltpu.CompilerParams(
            dimension_semantics=("parallel",), vmem_limit_bytes=VMEM_LIMIT),
        name="mix_a",
    )(h, g.reshape(1, D_MODEL), w_in, ln_g.reshape(1, D_INNER), ln_b.reshape(1, D_INNER),
      ws, bs.T, kt, vv, *[w for w, _, _ in cast])


def _mix_b_kernel(h_ref, g_ref, wzx_ref, wdt_ref, wdtt_ref, wq_ref, cw_ref, cb_ref,
                  dtb_ref, dtbt_ref, alog_ref, alogt_ref, dexp_ref, gn_ref, kt_ref, v_ref, *refs):
    n_cast = (len(refs) - 5) // 2
    o_ref = refs[n_cast]
    state_ref, ext_ref, xbc_ref, y_ref = refs[-4:]
    _cast_blocks(refs[:n_cast] + refs[n_cast + 1:-4])
    rows = h_ref.shape[0]
    n_chunks = rows // CHUNK
    pairs_per_group = SSM_HPG // 2
    x_blocks_per_group = SSM_GROUP_W // CONV_COLS

    @pl.when(pl.program_id(1) == 0)
    def _():
        state_ref[...] = jnp.zeros_like(state_ref)
        ext_ref[0:SUBLANES, :] = jnp.zeros((SUBLANES, CONV_DIM), F32)

    a = _rms(h_ref[...], g_ref[...]).astype(BF16)

    def conv_block(j):
        cols = slice(j * CONV_COLS, (j + 1) * CONV_COLS)
        ext_ref[SUBLANES:SUBLANES + rows, cols] = _mm(
            a, wzx_ref[:, D_INNER + j * CONV_COLS:D_INNER + (j + 1) * CONV_COLS])
        x = ext_ref[:, cols]
        acc = x[SUBLANES:] * cw_ref[CONV_K - 1:CONV_K, cols]
        for s in range(1, CONV_K):
            acc = acc + pltpu.roll(x, s, 0)[SUBLANES:] * cw_ref[CONV_K - 1 - s:CONV_K - s, cols]
        xbc_ref[:, cols] = jax.nn.silu(acc + cb_ref[:, cols])
        ext_ref[0:SUBLANES, cols] = x[rows:]

    for j in range(D_INNER // CONV_COLS, CONV_DIM // CONV_COLS):
        conv_block(j)
    for j in range(x_blocks_per_group):
        conv_block(j)

    dt = _softplus(_mm(a, wdt_ref[...]) + dtb_ref[...])
    dtt = _softplus(_mm_nt(wdtt_ref[...], a) + dtbt_ref[...])
    da = dt * -jnp.exp(alog_ref[...])
    dat = dtt * -jnp.exp(alogt_ref[...])

    l_idx = lax.broadcasted_iota(jnp.int32, (CHUNK, CHUNK), 0)
    s_idx = lax.broadcasted_iota(jnp.int32, (CHUNK, CHUNK), 1)
    causal = l_idx >= s_idx
    lower_ones = jnp.where(causal, 1.0, 0.0).astype(F32)
    upper_ones = jnp.where(l_idx <= s_idx, 1.0, 0.0).astype(F32)
    first_head = s_idx < SSM_HEAD_DIM
    first_head_row = first_head[0:1, :]

    chunk_vals = []
    for c in range(n_chunks):
        rsl = slice(c * CHUNK, (c + 1) * CHUNK)
        cs = _mm_f32(lower_ones, da[rsl, :])
        cst = _mm_f32(dat[:, rsl], upper_ones)
        dtt_c = dtt[:, rsl]
        chunk_vals.append(dict(
            rsl=rsl, cs=cs, cst=cst, dtt=dtt_c,
            cdec=jnp.exp(cs[CHUNK - 1:CHUNK, :]),
            wst=jnp.exp(cst[:, CHUNK - 1:CHUNK] - cst) * dtt_c))

    def group_operands(c, g):
        rsl = chunk_vals[c]["rsl"]
        b0 = D_INNER + g * SSM_STATE
        c0 = D_INNER + SSM_GROUPS * SSM_STATE + g * SSM_STATE
        bm = xbc_ref[rsl, b0:b0 + SSM_STATE]
        cm = xbc_ref[rsl, c0:c0 + SSM_STATE].astype(BF16)
        return dict(cm=cm, cb=_mm_nt(cm, bm.astype(BF16)),
                    bmt=bm.T)

    def ssd_pair(c, g, r, gv):
        cv = chunk_vals[c]
        rsl, cs, cst = cv["rsl"], cv["cs"], cv["cst"]
        p = g * pairs_per_group + r
        h0, h1 = 2 * p, 2 * p + 1
        cols = slice(p * PAIR_W, (p + 1) * PAIR_W)
        pcols = slice(r * PAIR_W, (r + 1) * PAIR_W)

        xs = xbc_ref[rsl, cols]
        rhs = jnp.concatenate([jnp.where(first_head, xs, 0.0).astype(BF16),
                               jnp.where(first_head, 0.0, xs).astype(BF16)], axis=0)

        csb0 = jnp.broadcast_to(cs[:, h0:h0 + 1], (CHUNK, CHUNK))
        csb1 = jnp.broadcast_to(cs[:, h1:h1 + 1], (CHUNK, CHUNK))

        def scores(csb, h):
            decay = jnp.exp(jnp.where(causal, csb - cst[h:h + 1, :], -jnp.inf))
            return (gv["cb"] * decay * cv["dtt"][h:h + 1, :]).astype(BF16)

        y_diag = _mm(jnp.concatenate([scores(csb0, h0), scores(csb1, h1)], axis=1), rhs)

        st = state_ref[g, :, pcols]
        y_off = _mm(gv["cm"], st.astype(BF16)) * jnp.where(first_head, jnp.exp(csb0), jnp.exp(csb1))
        y_ref[rsl, cols] = y_diag + y_off + xs * dexp_ref[:, cols]

        dec = jnp.where(first_head_row, cv["cdec"][:, h0:h0 + 1], cv["cdec"][:, h1:h1 + 1])
        bts = jnp.concatenate([(gv["bmt"] * cv["wst"][h0:h0 + 1, :]).astype(BF16),
                               (gv["bmt"] * cv["wst"][h1:h1 + 1, :]).astype(BF16)], axis=1)
        state_ref[g, :, pcols] = st * dec + _mm(bts, rhs)

    z_parts = {}

    def z_block(j):
        cols = slice(j * CONV_COLS, (j + 1) * CONV_COLS)
        z_parts[j] = jax.nn.silu(_mm(a, wzx_ref[:, cols]))

    def gate(g):
        cols = slice(g * SSM_GROUP_W, (g + 1) * SSM_GROUP_W)
        zs = jnp.concatenate([z_parts.pop(g * x_blocks_per_group + k) for k in range(x_blocks_per_group)],
                             axis=1)
        yg = y_ref[:, cols] * zs
        yg = yg * lax.rsqrt(jnp.mean(yg * yg, axis=-1, keepdims=True) + EPS)
        o_ref[:, cols] = (yg * gn_ref[:, cols]).astype(o_ref.dtype)

    def attend(hh):
        sl = slice(hh * X_HEAD_DIM, (hh + 1) * X_HEAD_DIM)
        o = _attend_head(_mm(a, wq_ref[:, sl]).astype(BF16), kt_ref, v_ref, hh)
        o_ref[:, D_INNER + hh * X_HEAD_DIM:D_INNER + (hh + 1) * X_HEAD_DIM] = o.astype(o_ref.dtype)

    assert SSM_GROUPS == X_HEADS
    group_vals = {}

    def prepare_group(g):
        group_vals[g] = [group_operands(c, g) for c in range(n_chunks)]

    prepare_group(0)
    for g in range(SSM_GROUPS):
        extras = []
        if g + 1 < SSM_GROUPS:
            extras += [functools.partial(conv_block, (g + 1) * x_blocks_per_group + k)
                       for k in range(x_blocks_per_group)]
            extras.append(functools.partial(prepare_group, g + 1))
        if g > 0:
            extras.append(functools.partial(gate, g - 1))
        extras += [functools.partial(z_block, g * x_blocks_per_group + k) for k in range(x_blocks_per_group)]
        extras.append(functools.partial(attend, g))
        for c in range(n_chunks):
            for r in range(pairs_per_group):
                ssd_pair(c, g, r, group_vals[g][c])
                if extras:
                    extras.pop(0)()
        while extras:
            extras.pop(0)()
    gate(SSM_GROUPS - 1)


def _mix_b(h, bn, g, wzx, w_in, conv_w, conv_b, dt_bias, a_log, d_skip, gnorm, kt, vv, cast):
    t = h.shape[0]
    tm = SCAN_TILE
    tiles_per_batch = (t // bn) // tm
    x1 = D_INNER + CONV_DIM
    d1 = x1 + SSM_HEADS
    wdt, wq = w_in[:, x1:d1].astype(BF16), w_in[:, d1:].astype(BF16)
    c_in, c_out, c_shapes = _cast_specs(cast, t // tm, lambda b, i: b * tiles_per_batch + i)
    return pl.pallas_call(
        _mix_b_kernel,
        grid=(bn, tiles_per_batch),
        in_specs=[
            pl.BlockSpec((tm, D_MODEL), lambda b, i: (b * tiles_per_batch + i, 0)),
            _resident((1, D_MODEL)),
            _resident((D_MODEL, D_INNER + CONV_DIM)),
            _resident((D_MODEL, SSM_HEADS)),
            _resident((SSM_HEADS, D_MODEL)),
            _resident((D_MODEL, X_WIDTH)),
            _resident((CONV_K, CONV_DIM)),
            _resident((1, CONV_DIM)),
            _resident((1, SSM_HEADS)),
            _resident((SSM_HEADS, 1)),
            _resident((1, SSM_HEADS)),
            _resident((SSM_HEADS, 1)),
            _resident((1, D_INNER)),
            _resident((1, D_INNER)),
            pl.BlockSpec((None, X_WIDTH, N_MEM), lambda b, i: (b, 0, 0)),
            pl.BlockSpec((None, N_MEM, X_WIDTH), lambda b, i: (b, 0, 0)),
        ] + c_in,
        out_specs=[pl.BlockSpec((tm, MIX_OUT), lambda b, i: (b * tiles_per_batch + i, 0))] + c_out,
        out_shape=[jax.ShapeDtypeStruct((t, MIX_OUT), BF16)] + c_shapes,
        scratch_shapes=[
            pltpu.VMEM((SSM_GROUPS, SSM_STATE, SSM_GROUP_W), F32),
            pltpu.VMEM((SUBLANES + tm, CONV_DIM), F32),
            pltpu.VMEM((tm, CONV_DIM), F32),
            pltpu.VMEM((tm, D_INNER), F32),
        ],
        compiler_params=pltpu.CompilerParams(
            dimension_semantics=("arbitrary", "arbitrary"), vmem_limit_bytes=VMEM_LIMIT),
        name="mix_b",
    )(h, g.reshape(1, D_MODEL), wzx, wdt, wdt.T, wq, conv_w, conv_b.reshape(1, CONV_DIM),
      dt_bias.reshape(1, SSM_HEADS), dt_bias.reshape(SSM_HEADS, 1),
      a_log.reshape(1, SSM_HEADS), a_log.reshape(SSM_HEADS, 1),
      jnp.repeat(d_skip, SSM_HEAD_DIM).reshape(1, D_INNER), gnorm.reshape(1, D_INNER), kt, vv,
      *[w for w, _, _ in cast])


def _post_kernel(h_ref, mc_ref, wo_ref, g_ref, w1_ref, w2_ref, fg_ref, *refs, final):
    n_cast = (len(refs) - 1) // 2
    o_ref = refs[n_cast]
    _cast_blocks(refs[:n_cast] + refs[n_cast + 1:])
    h = h_ref[...] + _mm(mc_ref[...], wo_ref[...])
    f = _rms(h, g_ref[...]).astype(BF16)
    for j in range(D_FF // FF_COLS):
        cols = slice(j * FF_COLS, (j + 1) * FF_COLS)
        t = jnp.square(jnp.maximum(_mm(f, w1_ref[:, cols]), 0.0)).astype(BF16)
        h = h + _mm(t, w2_ref[cols, :])
    if final:
        h = _rms(h, fg_ref[...])
    o_ref[...] = h


def _post(h, mixcat, w_out, g, w1, w2, final_g, final, cast):
    t = h.shape[0]
    tm = TOKEN_TILE
    c_in, c_out, c_shapes = _cast_specs(cast, t // tm, lambda i: i)
    return pl.pallas_call(
        functools.partial(_post_kernel, final=final),
        grid=(t // tm,),
        in_specs=[
            pl.BlockSpec((tm, D_MODEL), lambda i: (i, 0)),
            pl.BlockSpec((tm, MIX_OUT), lambda i: (i, 0)),
            _resident((MIX_OUT, D_MODEL)),
            _resident((1, D_MODEL)),
            _resident((D_MODEL, D_FF)),
            _resident((D_FF, D_MODEL)),
            _resident((1, D_MODEL)),
        ] + c_in,
        out_specs=[pl.BlockSpec((tm, D_MODEL), lambda i: (i, 0))] + c_out,
        out_shape=[jax.ShapeDtypeStruct((t, D_MODEL), F32)] + c_shapes,
        compiler_params=pltpu.CompilerParams(
            dimension_semantics=("parallel",), vmem_limit_bytes=VMEM_LIMIT),
        name="post_final" if final else "post",
    )(h, mixcat, w_out, g.reshape(1, D_MODEL), w1, w2, final_g.reshape(1, D_MODEL),
      *[w for w, _, _ in cast])


def kernel(x, mem, norm_mix, norm_ffn, mem_norm, w_kv, w_out, w_ffn1, w_ffn2,
           a_in, a_ln_g, a_ln_b, a_ws, a_bs,
           b_in, b_conv_w, b_conv_b, b_dt_bias, b_a_log, b_d, b_gnorm,
           final_norm):
    bn, s, d = x.shape
    depth = w_out.shape[0]
    assert d == D_MODEL and s % TOKEN_TILE == 0 and s % SCAN_TILE == 0
    assert TOKEN_TILE % CHUNK == 0 and SCAN_TILE % CHUNK == 0
    def mixer_weights(i):
        if i % N_MIXERS == 0:
            return [(a_in, i // N_MIXERS, 2 * D_INNER + X_WIDTH)]
        return [(b_in, i // N_MIXERS, D_INNER + CONV_DIM)]

    def post_weights(i):
        return [(w_out, i, D_MODEL), (w_ffn1, i, D_FF), (w_ffn2, i, D_MODEL)]

    kt, vv, *w_mix = _memory_kv(mem, mem_norm, w_kv, mixer_weights(0))
    h = x.reshape(bn * s, d)
    for i in range(depth):
        j = i // N_MIXERS
        if i % N_MIXERS == 0:
            mixcat, *w_post = _mix_a(h, bn, norm_mix[i], w_mix[0], a_ln_g[j], a_ln_b[j], a_ws[j], a_bs[j],
                                     kt[i], vv[i], post_weights(i))
        else:
            mixcat, *w_post = _mix_b(h, bn, norm_mix[i], w_mix[0], b_in[j], b_conv_w[j], b_conv_b[j],
                                     b_dt_bias[j], b_a_log[j], b_d[j], b_gnorm[j], kt[i], vv[i],
                                     post_weights(i))
        last = i == depth - 1
        h, *w_mix = _post(h, mixcat, w_post[0], norm_ffn[i], w_post[1], w_post[2], final_norm, last,
                          [] if last else mixer_weights(i + 1))
    return h.reshape(bn, s, d)
```

```python
import functools
import math

import jax
import jax.numpy as jnp
from jax import lax
from jax.experimental import pallas as pl
from jax.experimental.pallas import tpu as pltpu

F32 = jnp.float32
BF16 = jnp.bfloat16

D_MODEL = 1024
N_MIXERS = 2
CHUNK = 128
N_MEM = 256
D_INNER = 2 * D_MODEL
A_GROUPS = 8
A_GROUP_W = D_INNER // A_GROUPS
SSM_HEAD_DIM = 64
SSM_HEADS = D_INNER // SSM_HEAD_DIM
SSM_GROUPS = 4
SSM_HPG = SSM_HEADS // SSM_GROUPS
SSM_STATE = 128
SSM_GROUP_W = SSM_HPG * SSM_HEAD_DIM
CONV_K = 4
CONV_DIM = D_INNER + 2 * SSM_GROUPS * SSM_STATE
X_HEADS = 4
X_HEAD_DIM = 256
X_WIDTH = X_HEADS * X_HEAD_DIM
MIX_OUT = D_INNER + X_WIDTH
D_FF = 4 * D_MODEL
EPS = 1e-6

LANES = 128
SUBLANES = 8
PAIR_W = 2 * SSM_HEAD_DIM
assert PAIR_W == LANES and SSM_STATE == LANES and CHUNK == LANES

TOKEN_TILE = 512
SCAN_TILE = 256
FF_COLS = 1024
CONV_COLS = 256
VMEM_LIMIT = 56 * 1024 * 1024


def _mm(a, b):
    return jnp.dot(a, b, preferred_element_type=F32)


def _mm_nt(a, b):
    return lax.dot_general(a, b, (((1,), (1,)), ((), ())), preferred_element_type=F32)


def _mm_f32(a, b):
    return jnp.dot(a, b, preferred_element_type=F32, precision=lax.Precision.HIGHEST)


def _rms(x, g):
    return x * lax.rsqrt(jnp.mean(x * x, axis=-1, keepdims=True) + EPS) * g


def _gelu(x):
    return 0.5 * x * (1.0 + lax.erf(x * math.sqrt(0.5)))


def _softplus(x):
    return jnp.maximum(x, 0.0) + jnp.log1p(jnp.exp(-jnp.abs(x)))


def _resident(shape):
    zeros = (0,) * len(shape)
    return pl.BlockSpec(shape, lambda *_: zeros, pipeline_mode=pl.Buffered(1))


def _cast_specs(weights, n_steps, step_index):
    in_specs, out_specs, out_shapes, plan = [], [], [], []
    for w, layer, windows in weights:
        rows = w.shape[1] // n_steps
        cols = max(first + width for first, width in windows)
        assert rows * n_steps == w.shape[1] and rows % 16 == 0
        assert cols == w.shape[2] or cols % LANES == 0
        in_specs.append(pl.BlockSpec((None, rows, cols), lambda *g, layer=layer: (layer, step_index(*g), 0)))
        for _, width in windows:
            out_specs.append(pl.BlockSpec((rows, width), lambda *g: (step_index(*g), 0)))
            out_shapes.append(jax.ShapeDtypeStruct((w.shape[1], width), BF16))
        plan.append(tuple(windows))
    return in_specs, out_specs, out_shapes, tuple(plan)


def _cast_blocks(refs, plan):
    outs = iter(refs[len(plan):])
    for src_ref, windows in zip(refs, plan):
        w = src_ref[...]
        for first, width in windows:
            next(outs)[...] = w[:, first:first + width].astype(BF16)


ZX_COLS = D_INNER + CONV_DIM
WT_BLOCK = 256


def _ssd_weight_specs(wt, layer, n_steps):
    zx_blocks = ZX_COLS // WT_BLOCK
    q_blocks = X_WIDTH // LANES
    q0 = ZX_COLS + SSM_HEADS
    assert ZX_COLS % WT_BLOCK == 0 and ZX_COLS % SSM_HEADS == 0 and q0 % SUBLANES == 0
    assert n_steps >= zx_blocks and n_steps >= q_blocks and wt.shape[1] == q0 + X_WIDTH

    def zx_block(i):
        return jnp.minimum(i, zx_blocks - 1)

    def q_block(i):
        return jnp.minimum(i, q_blocks - 1)

    in_specs = [
        pl.BlockSpec((None, WT_BLOCK, D_MODEL), lambda i: (layer, zx_block(i), 0)),
        pl.BlockSpec((None, SSM_HEADS, D_MODEL), lambda i: (layer, ZX_COLS // SSM_HEADS, 0)),
        pl.BlockSpec((pl.Element(1), pl.Element(LANES), pl.Element(D_MODEL)),
                     lambda i: (layer, pl.multiple_of(q0 + LANES * q_block(i), SUBLANES), 0)),
    ]
    out_specs = [
        pl.BlockSpec((D_MODEL, WT_BLOCK), lambda i: (0, zx_block(i))),
        pl.BlockSpec((SSM_HEADS, D_MODEL), lambda i: (0, 0)),
        pl.BlockSpec((D_MODEL, LANES), lambda i: (0, q_block(i))),
    ]
    out_shapes = [
        jax.ShapeDtypeStruct((D_MODEL, ZX_COLS), BF16),
        jax.ShapeDtypeStruct((SSM_HEADS, D_MODEL), BF16),
        jax.ShapeDtypeStruct((D_MODEL, X_WIDTH), BF16),
    ]
    return in_specs, out_specs, out_shapes


def _ssd_weight_blocks(zx_t, dt_t, q_t, zx_o, dtt_o, q_o):
    zx_o[...] = zx_t[...].T.astype(BF16)
    dtt_o[...] = dt_t[...].astype(BF16)
    q_o[...] = q_t[0].T.astype(BF16)


def _attend_head(q, kt_ref, v_ref, hh):
    sl = slice(hh * X_HEAD_DIM, (hh + 1) * X_HEAD_DIM)
    sc = _mm(q, kt_ref[sl, :])
    e = jnp.exp(sc - jnp.max(sc, axis=-1, keepdims=True))
    return _mm(e.astype(BF16), v_ref[:, sl]) / jnp.sum(e, axis=-1, keepdims=True)


def _mem_attention(q, kt_ref, v_ref, o_ref, col0):
    qb = q.astype(BF16)
    for hh in range(X_HEADS):
        sl = slice(hh * X_HEAD_DIM, (hh + 1) * X_HEAD_DIM)
        o = _attend_head(qb[:, sl], kt_ref, v_ref, hh)
        o_ref[:, col0 + hh * X_HEAD_DIM:col0 + (hh + 1) * X_HEAD_DIM] = o.astype(o_ref.dtype)


def _kv_kernel(mem_ref, g_ref, w_ref, *refs, cast_plan):
    n_cast = len(cast_plan)
    kt_ref, v_ref = refs[n_cast:n_cast + 2]
    _cast_blocks(refs[:n_cast] + refs[n_cast + 2:], cast_plan)
    m = _rms(mem_ref[...], g_ref[...]).astype(BF16)
    kv = _mm(m, w_ref[...].astype(BF16))
    scale = 1.0 / math.sqrt(X_HEAD_DIM)
    kt_ref[...] = (kv[:, :X_WIDTH] * scale).T.astype(BF16)
    v_ref[...] = kv[:, X_WIDTH:].astype(BF16)


def _memory_kv(mem, mem_norm, w_kv, cast):
    depth = w_kv.shape[0]
    bn = mem.shape[0]
    c_in, c_out, c_shapes, plan = _cast_specs(cast, depth * bn, lambda l, b: l * bn + b)
    return pl.pallas_call(
        functools.partial(_kv_kernel, cast_plan=plan),
        grid=(depth, bn),
        in_specs=[
            pl.BlockSpec((None, N_MEM, D_MODEL), lambda l, b: (b, 0, 0)),
            pl.BlockSpec((None, 1, D_MODEL), lambda l, b: (l, 0, 0)),
            pl.BlockSpec((None, D_MODEL, 2 * X_WIDTH), lambda l, b: (l, 0, 0)),
        ] + c_in,
        out_specs=[
            pl.BlockSpec((None, None, X_WIDTH, N_MEM), lambda l, b: (l, b, 0, 0)),
            pl.BlockSpec((None, None, N_MEM, X_WIDTH), lambda l, b: (l, b, 0, 0)),
        ] + c_out,
        out_shape=[
            jax.ShapeDtypeStruct((depth, bn, X_WIDTH, N_MEM), BF16),
            jax.ShapeDtypeStruct((depth, bn, N_MEM, X_WIDTH), BF16),
        ] + c_shapes,
        compiler_params=pltpu.CompilerParams(
            dimension_semantics=("parallel", "parallel"), vmem_limit_bytes=VMEM_LIMIT),
        name="kv",
    )(mem, mem_norm.reshape(depth, 1, D_MODEL), w_kv, *[w for w, _, _ in cast])


def _mix_a_kernel(h_ref, g_ref, w_ref, lng_ref, lnb_ref, ws_ref, bst_ref, kt_ref, v_ref, *refs,
                  cast_plan):
    n_cast = len(cast_plan)
    o_ref = refs[n_cast]
    _cast_blocks(refs[:n_cast] + refs[n_cast + 1:], cast_plan)
    rows = h_ref.shape[0]
    a = _rms(h_ref[...], g_ref[...]).astype(BF16)
    u = _gelu(_mm(a, w_ref[:, 0:D_INNER]))
    v = _gelu(_mm(a, w_ref[:, D_INNER:2 * D_INNER]))
    vc = v - jnp.mean(v, axis=-1, keepdims=True)
    vn = vc * lax.rsqrt(jnp.mean(vc * vc, axis=-1, keepdims=True) + EPS)
    vn = (vn * lng_ref[...] + lnb_ref[...]).astype(BF16)

    t_idx = lax.broadcasted_iota(jnp.int32, (CHUNK, CHUNK), 0)
    s_idx = lax.broadcasted_iota(jnp.int32, (CHUNK, CHUNK), 1)
    causal = t_idx >= s_idx
    for g in range(A_GROUPS):
        cols = slice(g * A_GROUP_W, (g + 1) * A_GROUP_W)
        wg = jnp.where(causal, ws_ref[g], 0.0).astype(BF16)
        bias = bst_ref[:, g:g + 1]
        for c in range(rows // CHUNK):
            rsl = slice(c * CHUNK, (c + 1) * CHUNK)
            sv = _mm(wg, vn[rsl, cols]) + bias
            o_ref[rsl, cols] = (u[rsl, cols] * sv).astype(o_ref.dtype)

    q = _mm(a, w_ref[:, 2 * D_INNER:2 * D_INNER + X_WIDTH])
    _mem_attention(q, kt_ref, v_ref, o_ref, D_INNER)


def _mix_a(h, bn, g, w_in, ln_g, ln_b, ws, bs, kt, vv, layer, cast):
    t = h.shape[0]
    tm = TOKEN_TILE
    tiles_per_batch = (t // bn) // tm
    c_in, c_out, c_shapes, plan = _cast_specs(cast, t // tm, lambda i: i)
    return pl.pallas_call(
        functools.partial(_mix_a_kernel, cast_plan=plan),
        grid=(t // tm,),
        in_specs=[
            pl.BlockSpec((tm, D_MODEL), lambda i: (i, 0)),
            _resident((1, D_MODEL)),
            _resident((D_MODEL, 2 * D_INNER + X_WIDTH)),
            _resident((1, D_INNER)),
            _resident((1, D_INNER)),
            _resident((A_GROUPS, CHUNK, CHUNK)),
            _resident((CHUNK, A_GROUPS)),
            pl.BlockSpec((None, None, X_WIDTH, N_MEM), lambda i: (layer, i // tiles_per_batch, 0, 0)),
            pl.BlockSpec((None, None, N_MEM, X_WIDTH), lambda i: (layer, i // tiles_per_batch, 0, 0)),
        ] + c_in,
        out_specs=[pl.BlockSpec((tm, MIX_OUT), lambda i: (i, 0))] + c_out,
        out_shape=[jax.ShapeDtypeStruct((t, MIX_OUT), BF16)] + c_shapes,
        compiler_params=pltpu.CompilerParams(
            dimension_semantics=("parallel",), vmem_limit_bytes=VMEM_LIMIT),
        name="mix_a",
    )(h, g.reshape(1, D_MODEL), w_in, ln_g.reshape(1, D_INNER), ln_b.reshape(1, D_INNER),
      ws, bs.T, kt, vv, *[w for w, _, _ in cast])


def _mix_b_kernel(h_ref, g_ref, wzx_ref, wdtt_ref, wq_ref, cw_ref, cb_ref,
                  dtb_ref, dtbt_ref, alog_ref, alogt_ref, dexp_ref, gn_ref, kt_ref, v_ref, *refs,
                  cast_plan):
    n_cast = len(cast_plan)
    o_ref = refs[n_cast]
    state_ref, ext_ref, xbc_ref, y_ref = refs[-4:]
    _cast_blocks(refs[:n_cast] + refs[n_cast + 1:-4], cast_plan)
    rows = h_ref.shape[0]
    n_chunks = rows // CHUNK
    pairs_per_group = SSM_HPG // 2
    x_blocks_per_group = SSM_GROUP_W // CONV_COLS

    @pl.when(pl.program_id(1) == 0)
    def _():
        state_ref[...] = jnp.zeros_like(state_ref)
        ext_ref[0:SUBLANES, :] = jnp.zeros((SUBLANES, CONV_DIM), F32)

    a = _rms(h_ref[...], g_ref[...]).astype(BF16)

    def conv_block(j):
        cols = slice(j * CONV_COLS, (j + 1) * CONV_COLS)
        ext_ref[SUBLANES:SUBLANES + rows, cols] = _mm(
            a, wzx_ref[:, D_INNER + j * CONV_COLS:D_INNER + (j + 1) * CONV_COLS])
        x = ext_ref[:, cols]
        acc = x[SUBLANES:] * cw_ref[CONV_K - 1:CONV_K, cols]
        for s in range(1, CONV_K):
            acc = acc + pltpu.roll(x, s, 0)[SUBLANES:] * cw_ref[CONV_K - 1 - s:CONV_K - s, cols]
        xbc_ref[:, cols] = jax.nn.silu(acc + cb_ref[:, cols])
        ext_ref[0:SUBLANES, cols] = x[rows:]

    for j in range(D_INNER // CONV_COLS, CONV_DIM // CONV_COLS):
        conv_block(j)
    for j in range(x_blocks_per_group):
        conv_block(j)

    dt = _softplus(_mm_nt(a, wdtt_ref[...]) + dtb_ref[...])
    dtt = _softplus(_mm_nt(wdtt_ref[...], a) + dtbt_ref[...])
    da = dt * -jnp.exp(alog_ref[...])
    dat = dtt * -jnp.exp(alogt_ref[...])

    l_idx = lax.broadcasted_iota(jnp.int32, (CHUNK, CHUNK), 0)
    s_idx = lax.broadcasted_iota(jnp.int32, (CHUNK, CHUNK), 1)
    causal = l_idx >= s_idx
    lower_ones = jnp.where(causal, 1.0, 0.0).astype(F32)
    upper_ones = jnp.where(l_idx <= s_idx, 1.0, 0.0).astype(F32)
    first_head = s_idx < SSM_HEAD_DIM
    first_head_row = first_head[0:1, :]

    chunk_vals = []
    for c in range(n_chunks):
        rsl = slice(c * CHUNK, (c + 1) * CHUNK)
        cs = _mm_f32(lower_ones, da[rsl, :])
        cst = _mm_f32(dat[:, rsl], upper_ones)
        dtt_c = dtt[:, rsl]
        chunk_vals.append(dict(
            rsl=rsl, cs=cs, cst=cst, dtt=dtt_c,
            cdec=jnp.exp(cs[CHUNK - 1:CHUNK, :]),
            wst=jnp.exp(cst[:, CHUNK - 1:CHUNK] - cst) * dtt_c))

    def group_operands(c, g):
        rsl = chunk_vals[c]["rsl"]
        b0 = D_INNER + g * SSM_STATE
        c0 = D_INNER + SSM_GROUPS * SSM_STATE + g * SSM_STATE
        bm = xbc_ref[rsl, b0:b0 + SSM_STATE]
        cm = xbc_ref[rsl, c0:c0 + SSM_STATE].astype(BF16)
        return dict(cm=cm, cb=_mm_nt(cm, bm.astype(BF16)),
                    bmt=bm.T)

    def ssd_pair(c, g, r, gv):
        cv = chunk_vals[c]
        rsl, cs, cst = cv["rsl"], cv["cs"], cv["cst"]
        p = g * pairs_per_group + r
        h0, h1 = 2 * p, 2 * p + 1
        cols = slice(p * PAIR_W, (p + 1) * PAIR_W)
        pcols = slice(r * PAIR_W, (r + 1) * PAIR_W)

        xs = xbc_ref[rsl, cols]
        rhs = jnp.concatenate([jnp.where(first_head, xs, 0.0).astype(BF16),
                               jnp.where(first_head, 0.0, xs).astype(BF16)], axis=0)

        csb0 = jnp.broadcast_to(cs[:, h0:h0 + 1], (CHUNK, CHUNK))
        csb1 = jnp.broadcast_to(cs[:, h1:h1 + 1], (CHUNK, CHUNK))

        def scores(csb, h):
            decay = jnp.exp(jnp.where(causal, csb - cst[h:h + 1, :], -jnp.inf))
            return (gv["cb"] * decay * cv["dtt"][h:h + 1, :]).astype(BF16)

        y_diag = _mm(jnp.concatenate([scores(csb0, h0), scores(csb1, h1)], axis=1), rhs)

        st = state_ref[g, :, pcols]
        y_off = _mm(gv["cm"], st.astype(BF16)) * jnp.where(first_head, jnp.exp(csb0), jnp.exp(csb1))
        y_ref[rsl, cols] = y_diag + y_off + xs * dexp_ref[:, cols]

        dec = jnp.where(first_head_row, cv["cdec"][:, h0:h0 + 1], cv["cdec"][:, h1:h1 + 1])
        bts = jnp.concatenate([(gv["bmt"] * cv["wst"][h0:h0 + 1, :]).astype(BF16),
                               (gv["bmt"] * cv["wst"][h1:h1 + 1, :]).astype(BF16)], axis=1)
        state_ref[g, :, pcols] = st * dec + _mm(bts, rhs)

    z_parts = {}

    def z_block(j):
        cols = slice(j * CONV_COLS, (j + 1) * CONV_COLS)
        z_parts[j] = jax.nn.silu(_mm(a, wzx_ref[:, cols]))

    def gate(g):
        cols = slice(g * SSM_GROUP_W, (g + 1) * SSM_GROUP_W)
        zs = jnp.concatenate([z_parts.pop(g * x_blocks_per_group + k) for k in range(x_blocks_per_group)],
                             axis=1)
        yg = y_ref[:, cols] * zs
        yg = yg * lax.rsqrt(jnp.mean(yg * yg, axis=-1, keepdims=True) + EPS)
        o_ref[:, cols] = (yg * gn_ref[:, cols]).astype(o_ref.dtype)

    def attend(hh):
        sl = slice(hh * X_HEAD_DIM, (hh + 1) * X_HEAD_DIM)
        o = _attend_head(_mm(a, wq_ref[:, sl]).astype(BF16), kt_ref, v_ref, hh)
        o_ref[:, D_INNER + hh * X_HEAD_DIM:D_INNER + (hh + 1) * X_HEAD_DIM] = o.astype(o_ref.dtype)

    assert SSM_GROUPS == X_HEADS
    group_vals = {}

    def prepare_group(g):
        group_vals[g] = [group_operands(c, g) for c in range(n_chunks)]

    prepare_group(0)
    for g in range(SSM_GROUPS):
        extras = []
        if g + 1 < SSM_GROUPS:
            extras += [functools.partial(conv_block, (g + 1) * x_blocks_per_group + k)
                       for k in range(x_blocks_per_group)]
            extras.append(functools.partial(prepare_group, g + 1))
        if g > 0:
            extras.append(functools.partial(gate, g - 1))
        extras += [functools.partial(z_block, g * x_blocks_per_group + k) for k in range(x_blocks_per_group)]
        extras.append(functools.partial(attend, g))
        for c in range(n_chunks):
            for r in range(pairs_per_group):
                ssd_pair(c, g, r, group_vals[g][c])
                if extras:
                    extras.pop(0)()
        while extras:
            extras.pop(0)()
    gate(SSM_GROUPS - 1)


def _mix_b(h, bn, g, wzx, wdtt, wq, conv_w, conv_b, dt_bias, a_log, d_skip, gnorm, kt, vv, layer, cast):
    t = h.shape[0]
    tm = SCAN_TILE
    tiles_per_batch = (t // bn) // tm
    c_in, c_out, c_shapes, plan = _cast_specs(cast, t // tm, lambda b, i: b * tiles_per_batch + i)
    return pl.pallas_call(
        functools.partial(_mix_b_kernel, cast_plan=plan),
        grid=(bn, tiles_per_batch),
        in_specs=[
            pl.BlockSpec((tm, D_MODEL), lambda b, i: (b * tiles_per_batch + i, 0)),
            _resident((1, D_MODEL)),
            _resident((D_MODEL, ZX_COLS)),
            _resident((SSM_HEADS, D_MODEL)),
            _resident((D_MODEL, X_WIDTH)),
            _resident((CONV_K, CONV_DIM)),
            _resident((1, CONV_DIM)),
            _resident((1, SSM_HEADS)),
            _resident((SSM_HEADS, 1)),
            _resident((1, SSM_HEADS)),
            _resident((SSM_HEADS, 1)),
            _resident((1, D_INNER)),
            _resident((1, D_INNER)),
            pl.BlockSpec((None, None, X_WIDTH, N_MEM), lambda b, i: (layer, b, 0, 0)),
            pl.BlockSpec((None, None, N_MEM, X_WIDTH), lambda b, i: (layer, b, 0, 0)),
        ] + c_in,
        out_specs=[pl.BlockSpec((tm, MIX_OUT), lambda b, i: (b * tiles_per_batch + i, 0))] + c_out,
        out_shape=[jax.ShapeDtypeStruct((t, MIX_OUT), BF16)] + c_shapes,
        scratch_shapes=[
            pltpu.VMEM((SSM_GROUPS, SSM_STATE, SSM_GROUP_W), F32),
            pltpu.VMEM((SUBLANES + tm, CONV_DIM), F32),
            pltpu.VMEM((tm, CONV_DIM), F32),
            pltpu.VMEM((tm, D_INNER), F32),
        ],
        compiler_params=pltpu.CompilerParams(
            dimension_semantics=("arbitrary", "arbitrary"), vmem_limit_bytes=VMEM_LIMIT),
        name="mix_b",
    )(h, g.reshape(1, D_MODEL), wzx, wdtt, wq, conv_w, conv_b.reshape(1, CONV_DIM),
      dt_bias.reshape(1, SSM_HEADS), dt_bias.reshape(SSM_HEADS, 1),
      a_log.reshape(1, SSM_HEADS), a_log.reshape(SSM_HEADS, 1),
      jnp.repeat(d_skip, SSM_HEAD_DIM).reshape(1, D_INNER), gnorm.reshape(1, D_INNER), kt, vv,
      *[w for w, _, _ in cast])


def _post_kernel(h_ref, mc_ref, wo_ref, g_ref, w1_ref, w2_ref, fg_ref, *refs, final, cast_plan, n_ssd):
    n_cast = len(cast_plan)
    o_ref = refs[n_cast + n_ssd]
    if n_ssd:
        _ssd_weight_blocks(*refs[n_cast:n_cast + n_ssd], *refs[len(refs) - n_ssd:])
    _cast_blocks(refs[:n_cast] + refs[n_cast + n_ssd + 1:len(refs) - n_ssd], cast_plan)
    h = h_ref[...] + _mm(mc_ref[...], wo_ref[...])
    f = _rms(h, g_ref[...]).astype(BF16)
    for j in range(D_FF // FF_COLS):
        cols = slice(j * FF_COLS, (j + 1) * FF_COLS)
        t = jnp.square(jnp.maximum(_mm(f, w1_ref[:, cols]), 0.0)).astype(BF16)
        h = h + _mm(t, w2_ref[cols, :])
    if final:
        h = _rms(h, fg_ref[...])
    o_ref[...] = h


def _post(h, mixcat, w_out, g, w1, w2, final_g, final, cast, ssd_wt=None):
    t = h.shape[0]
    tm = TOKEN_TILE
    c_in, c_out, c_shapes, plan = _cast_specs(cast, t // tm, lambda i: i)
    s_in, s_out, s_shapes = _ssd_weight_specs(*ssd_wt, t // tm) if ssd_wt else ([], [], [])
    c_in, c_out, c_shapes = c_in + s_in, c_out + s_out, c_shapes + s_shapes
    ssd_args = [ssd_wt[0]] * len(s_in) if ssd_wt else []
    return pl.pallas_call(
        functools.partial(_post_kernel, final=final, cast_plan=plan, n_ssd=len(s_in)),
        grid=(t // tm,),
        in_specs=[
            pl.BlockSpec((tm, D_MODEL), lambda i: (i, 0)),
            pl.BlockSpec((tm, MIX_OUT), lambda i: (i, 0)),
            _resident((MIX_OUT, D_MODEL)),
            _resident((1, D_MODEL)),
            _resident((D_MODEL, D_FF)),
            _resident((D_FF, D_MODEL)),
            _resident((1, D_MODEL)),
        ] + c_in,
        out_specs=[pl.BlockSpec((tm, D_MODEL), lambda i: (i, 0))] + c_out,
        out_shape=[jax.ShapeDtypeStruct((t, D_MODEL), F32)] + c_shapes,
        compiler_params=pltpu.CompilerParams(
            dimension_semantics=("arbitrary",), vmem_limit_bytes=VMEM_LIMIT),
        name="post_final" if final else "post",
    )(h, mixcat, w_out, g.reshape(1, D_MODEL), w1, w2, final_g.reshape(1, D_MODEL),
      *[w for w, _, _ in cast], *ssd_args)


def kernel(x, mem, norm_mix, norm_ffn, mem_norm, w_kv, w_out, w_ffn1, w_ffn2,
           a_in, a_ln_g, a_ln_b, a_ws, a_bs,
           b_in, b_conv_w, b_conv_b, b_dt_bias, b_a_log, b_d, b_gnorm,
           final_norm):
    bn, s, d = x.shape
    depth = w_out.shape[0]
    assert d == D_MODEL and s % TOKEN_TILE == 0 and s % SCAN_TILE == 0
    assert TOKEN_TILE % CHUNK == 0 and SCAN_TILE % CHUNK == 0
    def mixer_weights(i):
        return [(a_in, i // N_MIXERS, ((0, 2 * D_INNER + X_WIDTH),))] if i % N_MIXERS == 0 else []

    b_in_t = jnp.swapaxes(b_in, 1, 2)

    def ssd_weights(i):
        return (b_in_t, i // N_MIXERS) if i % N_MIXERS == 1 else None

    def post_weights(i):
        return [(w_out, i, ((0, D_MODEL),)), (w_ffn1, i, ((0, D_FF),)), (w_ffn2, i, ((0, D_MODEL),))]

    kt, vv, *w_mix = _memory_kv(mem, mem_norm, w_kv, mixer_weights(0))
    h = x.reshape(bn * s, d)
    for i in range(depth):
        j = i // N_MIXERS
        if i % N_MIXERS == 0:
            mixcat, *w_post = _mix_a(h, bn, norm_mix[i], w_mix[0], a_ln_g[j], a_ln_b[j], a_ws[j], a_bs[j],
                                     kt, vv, i, post_weights(i))
        else:
            mixcat, *w_post = _mix_b(h, bn, norm_mix[i], w_mix[0], w_mix[1], w_mix[2], b_conv_w[j], b_conv_b[j],
                                     b_dt_bias[j], b_a_log[j], b_d[j], b_gnorm[j], kt, vv, i,
                                     post_weights(i))
        last = i == depth - 1
        h, *w_mix = _post(h, mixcat, w_post[0], norm_ffn[i], w_post[1], w_post[2], final_norm, last,
                          [] if last else mixer_weights(i + 1), None if last else ssd_weights(i + 1))
    return h.reshape(bn, s, d)
```
